```python
import jax, jax.numpy as jnp
from jax import lax
import numpy as np

D_MODEL = 1024
BATCH = 16
SEQ = 4096
DEPTH = 1
DEC_BATCH = 2
DEC_SEQ = 8192
PAST_LEN = 128

MIX_WIDTH = D_MODEL
POOL_WIDTH = MIX_WIDTH // 2
ATTN_WIDTH = MIX_WIDTH - POOL_WIDTH
POOL_WINDOWS = (2, 4, 8, 16)
N_POOL_GROUPS = len(POOL_WINDOWS)
POOL_GROUP_DIM = POOL_WIDTH // N_POOL_GROUPS
HEAD_DIM = 64
N_HEADS = ATTN_WIDTH // HEAD_DIM
GRID_W = 64
WIN_ROWS_MAX = 8
WIN_COLS = 16
RPB_ROWS = 2 * WIN_ROWS_MAX - 1
RPB_COLS = 2 * WIN_COLS - 1
IN_WIDTH = 2 * POOL_WIDTH + 4 * ATTN_WIDTH
EPS = 1e-6

kernel_name = "hymba_pool_natten_encoder"


def rms_norm(x, g):
    xf = x.astype(jnp.float32)
    y = xf * lax.rsqrt(jnp.mean(xf * xf, axis=-1, keepdims=True) + EPS)
    return (y * g.astype(jnp.float32)).astype(x.dtype)


def centred_window_mean(u, w):
    B, S, C = u.shape
    a = w // 2
    b = w - a
    csum = jnp.cumsum(u.astype(jnp.float32), axis=1)
    csum = jnp.concatenate([jnp.zeros((B, 1, C), jnp.float32), csum], axis=1)
    cpad = jnp.pad(csum, ((0, 0), (a, b), (0, 0)), mode="edge")
    window_sum = cpad[:, w:w + S] - cpad[:, 0:S]
    t = jnp.arange(S)
    count = (jnp.minimum(t + b, S) - jnp.maximum(t - a, 0)).astype(jnp.float32)
    return window_sum / count[None, :, None]


def pool_mixer(u, w_pool, pool_scale):
    B, S, _ = u.shape
    ug = u.reshape(B, S, N_POOL_GROUPS, POOL_GROUP_DIM)
    pooled = jnp.stack(
        [centred_window_mean(ug[:, :, g], w) - ug[:, :, g].astype(jnp.float32)
         for g, w in enumerate(POOL_WINDOWS)], axis=2)
    mixed = jnp.einsum("bsgc,gcd->bsgd", pooled.astype(u.dtype), w_pool)
    return mixed.reshape(B, S, POOL_WIDTH) * pool_scale


def neighbourhood_attention(q, k, v, rpb):
    B, S, H, Dh = q.shape
    rows = S // GRID_W
    kr = min(WIN_ROWS_MAX, rows)
    qg = q.reshape(B, rows, GRID_W, H, Dh)
    kg = k.reshape(B, rows, GRID_W, H, Dh)
    vg = v.reshape(B, rows, GRID_W, H, Dh)
    col = np.arange(GRID_W)
    col_start = np.clip(col - WIN_COLS // 2, 0, GRID_W - WIN_COLS)
    col_idx = col_start[:, None] + np.arange(WIN_COLS)[None, :]
    col_off = col_idx - col[:, None] + (WIN_COLS - 1)
    bias_cols = rpb[:, :, col_off]

    def row_block(r):
        rs = jnp.clip(r - kr // 2, 0, rows - kr)
        q_r = lax.dynamic_index_in_dim(qg, r, axis=1, keepdims=False)
        k_r = lax.dynamic_slice_in_dim(kg, rs, kr, axis=1)
        v_r = lax.dynamic_slice_in_dim(vg, rs, kr, axis=1)
        k_w = k_r[:, :, col_idx]
        v_w = v_r[:, :, col_idx]
        s = jnp.einsum("bchd,bicjhd->bhcij", q_r, k_w).astype(jnp.float32)
        row_off = rs + jnp.arange(kr) - r + (WIN_ROWS_MAX - 1)
        bias = jnp.take(bias_cols, row_off, axis=1)
        s = s + jnp.transpose(bias, (0, 2, 1, 3)).astype(jnp.float32)[None]
        p = jax.nn.softmax(s.reshape(B, H, GRID_W, kr * WIN_COLS), axis=-1)
        p = p.reshape(B, H, GRID_W, kr, WIN_COLS).astype(v.dtype)
        return jnp.einsum("bhcij,bicjhd->bchd", p, v_w)

    out = lax.map(row_block, jnp.arange(rows))
    return jnp.transpose(out, (1, 0, 2, 3, 4)).reshape(B, S, H * Dh)


def encoder_layer(x, norm_g, w_in, w_pool, pool_scale, q_norm_g, k_norm_g, rpb, w_out):
    B, S, _ = x.shape
    h = rms_norm(x, norm_g)
    proj = h @ w_in
    P, A = POOL_WIDTH, ATTN_WIDTH
    u_pool, g_pool, q, k, v, g_attn = jnp.split(
        proj, [P, 2 * P, 2 * P + A, 2 * P + 2 * A, 2 * P + 3 * A], axis=-1)
    pool_out = pool_mixer(u_pool, w_pool, pool_scale) * jax.nn.silu(g_pool)
    q = rms_norm(q.reshape(B, S, N_HEADS, HEAD_DIM), q_norm_g) * (HEAD_DIM ** -0.5)
    k = rms_norm(k.reshape(B, S, N_HEADS, HEAD_DIM), k_norm_g)
    v = v.reshape(B, S, N_HEADS, HEAD_DIM)
    attn_out = neighbourhood_attention(q, k, v, rpb) * jax.nn.silu(g_attn)
    mixed = jnp.concatenate([pool_out, attn_out], axis=-1)
    return x + mixed @ w_out


def setup_inputs(seed: int = 0) -> dict:
    key = jax.random.key(seed)
    ks = jax.random.split(key, 11)
    f32 = jnp.float32
    x_prompt = jax.random.normal(ks[0], (BATCH, SEQ, D_MODEL), f32)
    x_sample = jax.random.normal(ks[1], (DEC_BATCH, DEC_SEQ, D_MODEL), f32)
    norm_g = 1.0 + 0.02 * jax.random.normal(ks[2], (DEPTH, D_MODEL), f32)
    w_in = jax.random.normal(ks[3], (DEPTH, D_MODEL, IN_WIDTH), f32) * D_MODEL ** -0.5
    w_pool = jax.random.normal(ks[4], (DEPTH, N_POOL_GROUPS, POOL_GROUP_DIM, POOL_GROUP_DIM), f32) * POOL_GROUP_DIM ** -0.5
    pool_scale = 1.0 + 0.02 * jax.random.normal(ks[5], (DEPTH, POOL_WIDTH), f32)
    q_norm_g = 1.0 + 0.02 * jax.random.normal(ks[6], (DEPTH, HEAD_DIM), f32)
    k_norm_g = 1.0 + 0.02 * jax.random.normal(ks[7], (DEPTH, HEAD_DIM), f32)
    rpb = 0.1 * jax.random.normal(ks[8], (DEPTH, N_HEADS, RPB_ROWS, RPB_COLS), f32)
    w_out = jax.random.normal(ks[9], (DEPTH, MIX_WIDTH, D_MODEL), f32) * MIX_WIDTH ** -0.5
    return {"x_prompt": x_prompt, "x_sample": x_sample, "norm_g": norm_g, "w_in": w_in,
            "w_pool": w_pool, "pool_scale": pool_scale, "q_norm_g": q_norm_g,
            "k_norm_g": k_norm_g, "rpb": rpb, "w_out": w_out}


def reference(x_prompt, x_sample, norm_g, w_in, w_pool, pool_scale, q_norm_g, k_norm_g, rpb, w_out):
    y_prompt = x_prompt
    y_sample = x_sample
    for l in range(DEPTH):
        y_prompt = encoder_layer(y_prompt, norm_g[l], w_in[l], w_pool[l], pool_scale[l],
                                 q_norm_g[l], k_norm_g[l], rpb[l], w_out[l])
        y_sample = encoder_layer(y_sample, norm_g[l], w_in[l], w_pool[l], pool_scale[l],
                                 q_norm_g[l], k_norm_g[l], rpb[l], w_out[l])
    return (y_prompt, y_sample)
```

```python
import functools

import jax
import jax.numpy as jnp
import numpy as np
from jax import lax
from jax.experimental import pallas as pl
from jax.experimental.pallas import tpu as pltpu

D_MODEL = 1024
POOL_WIDTH = 512
ATTN_WIDTH = 512
POOL_WINDOWS = (2, 4, 8, 16)
POOL_GROUP_DIM = POOL_WIDTH // len(POOL_WINDOWS)
HEAD_DIM = 64
N_HEADS = ATTN_WIDTH // HEAD_DIM
GRID_W = 64
WIN_ROWS = 8
WIN_COLS = 16
IN_WIDTH = 2 * POOL_WIDTH + 4 * ATTN_WIDTH
EPS = 1e-6

LANES = 128
N_PAIRS = ATTN_WIDTH // LANES
HALO = 16
MASKED = -1e30

PROJ_TOKENS = 512
BLOCK_ROWS = 8
BLOCK_TOKENS = BLOCK_ROWS * GRID_W
VMEM_LIMIT = 56 * 1024 * 1024

F32 = jnp.float32
BF16 = jnp.bfloat16


def _resident(shape, index_map):
    return pl.BlockSpec(shape, index_map, pipeline_mode=pl.Buffered(1))


def _silu(g):
    return g * (1.0 / (1.0 + jnp.exp(-g)))


def _head_rms_norm(blk, gain, low_half):
    sq = blk * blk
    s_lo = jnp.sum(jnp.where(low_half, sq, 0.0), axis=-1, keepdims=True)
    s_hi = jnp.sum(jnp.where(low_half, 0.0, sq), axis=-1, keepdims=True)
    ms = jnp.where(low_half, s_lo, s_hi) * (1.0 / HEAD_DIM)
    return blk * lax.rsqrt(ms + EPS) * gain


def _proj_kernel(x_ref, ng_ref, win_ref, qg_ref, kg_ref,
                 u_ref, gp_ref, q_ref, k_ref, v_ref, ga_ref):
    x = x_ref[...]
    ms = jnp.mean(x * x, axis=-1, keepdims=True)
    h = (x * lax.rsqrt(ms + EPS) * ng_ref[...]).astype(BF16)

    def seg(i):
        return jnp.dot(h, win_ref[:, i * 512:(i + 1) * 512], preferred_element_type=F32)

    u_ref[...] = seg(0).astype(BF16)
    gp_ref[...] = _silu(seg(1)).astype(BF16)
    low_half = lax.broadcasted_iota(jnp.int32, (x.shape[0], LANES), 1) < HEAD_DIM
    q = seg(2)
    for p in range(N_PAIRS):
        c = slice(p * LANES, (p + 1) * LANES)
        qn = _head_rms_norm(q[:, c], qg_ref[:, c], low_half) * (HEAD_DIM ** -0.5)
        q_ref[:, c] = qn.astype(BF16)
    k = seg(3)
    for p in range(N_PAIRS):
        c = slice(p * LANES, (p + 1) * LANES)
        k_ref[:, c] = _head_rms_norm(k[:, c], kg_ref[:, c], low_half).astype(BF16)
    v_ref[...] = seg(4).astype(BF16)
    ga_ref[...] = _silu(seg(5)).astype(BF16)


def _proj_call(x2d, norm_g, w_in_bf, qg, kg):
    n = x2d.shape[0]
    tm = PROJ_TOKENS
    out = jax.ShapeDtypeStruct((n, 512), BF16)
    row = lambda i: (i, 0)
    fixed = lambda i: (0, 0)
    return pl.pallas_call(
        _proj_kernel,
        grid=(n // tm,),
        in_specs=[
            pl.BlockSpec((tm, D_MODEL), row),
            _resident((1, D_MODEL), fixed),
            _resident((D_MODEL, IN_WIDTH), fixed),
            _resident((1, ATTN_WIDTH), fixed),
            _resident((1, ATTN_WIDTH), fixed),
        ],
        out_specs=[pl.BlockSpec((tm, 512), row)] * 6,
        out_shape=[out] * 6,
        compiler_params=pltpu.CompilerParams(
            dimension_semantics=("arbitrary",), vmem_limit_bytes=VMEM_LIMIT),
        name="proj",
    )(x2d, norm_g, w_in_bf, qg, kg)


def _mixer_kernel(x_ref, q_ref, k_ref, v_ref, ga_ref, u_ref, uprev_ref, unext_ref, gp_ref,
                  bias_ref, wpool_ref, pscale_ref, wout_ref, y_ref, mixed_ref, uext_ref,
                  *, seq_len):
    j = pl.program_id(1)
    nj = pl.num_programs(1)
    rows = seq_len // GRID_W
    tb = BLOCK_TOKENS
    low_half = lax.broadcasted_iota(jnp.int32, (GRID_W, LANES), 1) < HEAD_DIM

    def attend_row(i, carry):
        r = j * BLOCK_ROWS + i
        rs = jnp.clip(r - WIN_ROWS // 2, 0, rows - WIN_ROWS)
        delta = r - rs
        k0 = pl.multiple_of(rs * GRID_W, GRID_W)
        q0 = pl.multiple_of(i * GRID_W, GRID_W)
        for p in range(N_PAIRS):
            c = slice(p * LANES, (p + 1) * LANES)
            q2 = q_ref[pl.ds(q0, GRID_W), c]
            zero = jnp.zeros_like(q2)
            qq = jnp.concatenate([jnp.where(low_half, q2, zero),
                                  jnp.where(low_half, zero, q2)], axis=0)
            k2 = k_ref[pl.ds(k0, WIN_ROWS * GRID_W), c]
            s = lax.dot_general(qq, k2, (((1,), (1,)), ((), ())),
                                preferred_element_type=F32)
            s = s + bias_ref[p, delta]
            m = jnp.max(s, axis=-1, keepdims=True)
            e = jnp.exp(s - m)
            l = jnp.sum(e, axis=-1, keepdims=True)
            v2 = v_ref[pl.ds(k0, WIN_ROWS * GRID_W), c]
            o = jnp.dot(e.astype(BF16), v2, preferred_element_type=F32) * (1.0 / l)
            o2 = jnp.where(low_half, o[:GRID_W], o[GRID_W:])
            gate = ga_ref[pl.ds(q0, GRID_W), c].astype(F32)
            mixed_ref[pl.ds(q0, GRID_W), POOL_WIDTH + p * LANES:POOL_WIDTH + (p + 1) * LANES] = (
                (o2 * gate).astype(BF16))
        return carry

    lax.fori_loop(0, BLOCK_ROWS, attend_row, 0)

    prev = uprev_ref[...].astype(F32)
    nxt = unext_ref[...].astype(F32)
    uext_ref[0:HALO, :] = jnp.where(j > 0, prev, jnp.zeros_like(prev))
    uext_ref[HALO:HALO + tb, :] = u_ref[...].astype(F32)
    uext_ref[HALO + tb:HALO + tb + HALO, :] = jnp.where(j < nj - 1, nxt, jnp.zeros_like(nxt))
    t = j * tb + lax.broadcasted_iota(jnp.int32, (tb, 1), 0)
    for g, w in enumerate(POOL_WINDOWS):
        a = w // 2
        b = w - a
        c = slice(g * POOL_GROUP_DIM, (g + 1) * POOL_GROUP_DIM)
        acc = uext_ref[HALO - a:HALO - a + tb, c]
        for d in range(-a + 1, b):
            acc = acc + uext_ref[HALO + d:HALO + d + tb, c]
        count = (jnp.minimum(t + b, seq_len) - jnp.maximum(t - a, 0)).astype(F32)
        pooled = acc / count - uext_ref[HALO:HALO + tb, c]
        mixed = jnp.dot(pooled.astype(BF16), wpool_ref[g], preferred_element_type=F32)
        out = mixed * pscale_ref[:, c] * gp_ref[:, c].astype(F32)
        mixed_ref[:, c] = out.astype(BF16)

    y_ref[...] = x_ref[...] + jnp.dot(mixed_ref[...], wout_ref[...], preferred_element_type=F32)


def _mixer_call(x, u, gp, q, k, v, ga, bias, wpool_bf, pscale, wout_bf):
    bsz, seq_len, _ = x.shape
    tb = BLOCK_TOKENS
    nj = seq_len // tb
    halo_blocks = tb // HALO
    n_halo = seq_len // HALO
    blk = lambda b, j: (b, j, 0)
    whole = lambda b, j: (b, 0, 0)
    return pl.pallas_call(
        functools.partial(_mixer_kernel, seq_len=seq_len),
        grid=(bsz, nj),
        in_specs=[
            pl.BlockSpec((None, tb, D_MODEL), blk),
            pl.BlockSpec((None, tb, ATTN_WIDTH), blk),
            _resident((None, seq_len, ATTN_WIDTH), whole),
            _resident((None, seq_len, ATTN_WIDTH), whole),
            pl.BlockSpec((None, tb, ATTN_WIDTH), blk),
            pl.BlockSpec((None, tb, POOL_WIDTH), blk),
            pl.BlockSpec((None, HALO, POOL_WIDTH),
                         lambda b, j: (b, jnp.maximum(j * halo_blocks - 1, 0), 0)),
            pl.BlockSpec((None, HALO, POOL_WIDTH),
                         lambda b, j: (b, jnp.minimum((j + 1) * halo_blocks, n_halo - 1), 0)),
            pl.BlockSpec((None, tb, POOL_WIDTH), blk),
            _resident(bias.shape, lambda b, j: (0, 0, 0, 0)),
            _resident(wpool_bf.shape, lambda b, j: (0, 0, 0)),
            _resident((1, POOL_WIDTH), lambda b, j: (0, 0)),
            _resident((D_MODEL, D_MODEL), lambda b, j: (0, 0)),
        ],
        out_specs=pl.BlockSpec((None, tb, D_MODEL), blk),
        out_shape=jax.ShapeDtypeStruct(x.shape, F32),
        scratch_shapes=[
            pltpu.VMEM((tb, D_MODEL), BF16),
            pltpu.VMEM((tb + 2 * HALO, POOL_WIDTH), F32),
        ],
        compiler_params=pltpu.CompilerParams(
            dimension_semantics=("arbitrary", "arbitrary"), vmem_limit_bytes=VMEM_LIMIT),
        name="mixer",
    )(x, q, k, v, ga, u, u, u, gp, bias, wpool_bf, pscale, wout_bf)


def _attention_bias(rpb):
    col = np.arange(GRID_W)
    col_start = np.clip(col - WIN_COLS // 2, 0, GRID_W - WIN_COLS)
    kc = np.arange(GRID_W)
    valid = (kc[None, :] >= col_start[:, None]) & (kc[None, :] < col_start[:, None] + WIN_COLS)
    col_off = np.clip(kc[None, :] - col[:, None] + (WIN_COLS - 1), 0, 2 * WIN_COLS - 2)
    table = jnp.where(valid[None, None], rpb[:, :, col_off], MASKED)
    row_off = np.arange(WIN_ROWS)[None, :] - np.arange(WIN_ROWS)[:, None] + (WIN_ROWS - 1)
    t = table[:, row_off]
    t = jnp.transpose(t, (0, 1, 3, 2, 4)).reshape(N_PAIRS, 2, WIN_ROWS, GRID_W, WIN_ROWS * GRID_W)
    t = jnp.transpose(t, (0, 2, 1, 3, 4))
    return t.reshape(N_PAIRS, WIN_ROWS, 2 * GRID_W, WIN_ROWS * GRID_W).astype(F32)


def _encoder_layer(x, norm_g, w_in, w_pool, pool_scale, q_norm_g, k_norm_g, rpb, w_out):
    bsz, seq_len, _ = x.shape
    assert seq_len % BLOCK_TOKENS == 0 and seq_len // GRID_W >= WIN_ROWS
    assert (bsz * seq_len) % PROJ_TOKENS == 0
    qg = jnp.tile(q_norm_g, N_HEADS)[None, :]
    kg = jnp.tile(k_norm_g, N_HEADS)[None, :]
    u, gp, q, k, v, ga = _proj_call(x.reshape(bsz * seq_len, D_MODEL), norm_g[None, :],
                                    w_in.astype(BF16), qg, kg)
    shp = (bsz, seq_len, 512)
    return _mixer_call(x, u.reshape(shp), gp.reshape(shp), q.reshape(shp), k.reshape(shp),
                       v.reshape(shp), ga.reshape(shp), _attention_bias(rpb),
                       w_pool.astype(BF16), pool_scale[None, :], w_out.astype(BF16))


def kernel(x_prompt, x_sample, norm_g, w_in, w_pool, pool_scale, q_norm_g, k_norm_g, rpb, w_out):
    y_prompt, y_sample = x_prompt, x_sample
    for l in range(norm_g.shape[0]):
        params = (norm_g[l], w_in[l], w_pool[l], pool_scale[l], q_norm_g[l], k_norm_g[l],
                  rpb[l], w_out[l])
        y_prompt = _encoder_layer(y_prompt, *params)
        y_sample = _encoder_layer(y_sample, *params)
    return (y_prompt, y_sample)
```

```python
import functools
import math

import jax
import jax.numpy as jnp
import numpy as np
from jax import lax
from jax.experimental import pallas as pl
from jax.experimental.pallas import tpu as pltpu

D_MODEL = 1024
POOL_WIDTH = 512
ATTN_WIDTH = 512
POOL_WINDOWS = (2, 4, 8, 16)
POOL_GROUP_DIM = POOL_WIDTH // len(POOL_WINDOWS)
HEAD_DIM = 64
N_HEADS = ATTN_WIDTH // HEAD_DIM
GRID_W = 64
WIN_ROWS = 8
WIN_COLS = 16
IN_WIDTH = 2 * POOL_WIDTH + 4 * ATTN_WIDTH
EPS = 1e-6
LOG2E = math.log2(math.e)

LANES = 128
N_PAIRS = ATTN_WIDTH // LANES
HALO = 16
MASKED = -1e30

PROJ_TOKENS = 512
BLOCK_ROWS = 8
BLOCK_TOKENS = BLOCK_ROWS * GRID_W
VMEM_LIMIT = 56 * 1024 * 1024

F32 = jnp.float32
BF16 = jnp.bfloat16


def _resident(shape, index_map):
    return pl.BlockSpec(shape, index_map, pipeline_mode=pl.Buffered(1))


def _silu(g):
    return g * (1.0 / (1.0 + jnp.exp(-g)))


def _head_rms_norm(blk, gain, low_half):
    sq = blk * blk
    s_lo = jnp.sum(jnp.where(low_half, sq, 0.0), axis=-1, keepdims=True)
    s_hi = jnp.sum(jnp.where(low_half, 0.0, sq), axis=-1, keepdims=True)
    ms = jnp.where(low_half, s_lo, s_hi) * (1.0 / HEAD_DIM)
    return blk * lax.rsqrt(ms + EPS) * gain


def _proj_kernel(x_ref, xprev_ref, xnext_ref, ng_ref, win_ref, qg_ref, kg_ref, wpool_ref,
                 pscale_ref, pool_ref, q_ref, k_ref, v_ref, ga_ref, uext_ref,
                 *, seq_len):
    tm = x_ref.shape[0]
    tiles_per_seq = seq_len // tm
    pos = pl.program_id(0) % tiles_per_seq

    def normed(x):
        ms = jnp.mean(x * x, axis=-1, keepdims=True)
        return (x * lax.rsqrt(ms + EPS) * ng_ref[...]).astype(BF16)

    h = normed(x_ref[...])

    def seg(i):
        return jnp.dot(h, win_ref[:, i * 512:(i + 1) * 512], preferred_element_type=F32)

    h_ext = jnp.concatenate([normed(xprev_ref[...]), h, normed(xnext_ref[...])], axis=0)
    u_ext = jnp.dot(h_ext, win_ref[:, 0:POOL_WIDTH], preferred_element_type=F32)
    uext_ref[0:HALO, :] = jnp.where(pos > 0, u_ext[0:HALO], 0.0)
    uext_ref[HALO:HALO + tm, :] = u_ext[HALO:HALO + tm]
    uext_ref[HALO + tm:, :] = jnp.where(pos < tiles_per_seq - 1, u_ext[HALO + tm:], 0.0)
    gate = _silu(seg(1))
    t = pos * tm + lax.broadcasted_iota(jnp.int32, (tm, 1), 0)
    pooled = []
    for g, w in enumerate(POOL_WINDOWS):
        a = w // 2
        b = w - a
        c = slice(g * POOL_GROUP_DIM, (g + 1) * POOL_GROUP_DIM)
        acc = uext_ref[HALO - a:HALO - a + tm, c]
        for d in range(-a + 1, b):
            acc = acc + uext_ref[HALO + d:HALO + d + tm, c]
        count = (jnp.minimum(t + b, seq_len) - jnp.maximum(t - a, 0)).astype(F32)
        pooled.append((acc / count - uext_ref[HALO:HALO + tm, c]).astype(BF16))

    low_half = lax.broadcasted_iota(jnp.int32, (tm, LANES), 1) < HEAD_DIM
    q = seg(2)
    for p in range(N_PAIRS):
        c = slice(p * LANES, (p + 1) * LANES)
        qn = _head_rms_norm(q[:, c], qg_ref[:, c], low_half) * (HEAD_DIM ** -0.5 * LOG2E)
        q_ref[:, c] = qn.astype(BF16)
    k = seg(3)
    for p in range(N_PAIRS):
        c = slice(p * LANES, (p + 1) * LANES)
        k_ref[:, c] = _head_rms_norm(k[:, c], kg_ref[:, c], low_half).astype(BF16)
    v_ref[...] = seg(4).astype(BF16)
    ga_ref[...] = _silu(seg(5)).astype(BF16)

    for g in range(len(POOL_WINDOWS)):
        c = slice(g * POOL_GROUP_DIM, (g + 1) * POOL_GROUP_DIM)
        mixed = jnp.dot(pooled[g], wpool_ref[g], preferred_element_type=F32)
        pool_ref[:, c] = (mixed * pscale_ref[:, c] * gate[:, c]).astype(BF16)


def _proj_call(x2d, seq_len, norm_g, w_in_bf, qg, kg, wpool_bf, pscale):
    n = x2d.shape[0]
    tm = PROJ_TOKENS
    halo_blocks = tm // HALO
    n_halo = n // HALO
    out = jax.ShapeDtypeStruct((n, 512), BF16)
    row = lambda i: (i, 0)
    fixed2 = lambda i: (0, 0)
    return pl.pallas_call(
        functools.partial(_proj_kernel, seq_len=seq_len),
        grid=(n // tm,),
        in_specs=[
            pl.BlockSpec((tm, D_MODEL), row),
            pl.BlockSpec((HALO, D_MODEL), lambda i: (jnp.maximum(i * halo_blocks - 1, 0), 0)),
            pl.BlockSpec((HALO, D_MODEL),
                         lambda i: (jnp.minimum((i + 1) * halo_blocks, n_halo - 1), 0)),
            _resident((1, D_MODEL), fixed2),
            _resident((D_MODEL, IN_WIDTH), fixed2),
            _resident((1, ATTN_WIDTH), fixed2),
            _resident((1, ATTN_WIDTH), fixed2),
            _resident(wpool_bf.shape, lambda i: (0, 0, 0)),
            _resident((1, POOL_WIDTH), fixed2),
        ],
        out_specs=[pl.BlockSpec((tm, 512), row)] * 5,
        out_shape=[out] * 5,
        scratch_shapes=[pltpu.VMEM((tm + 2 * HALO, POOL_WIDTH), F32)],
        compiler_params=pltpu.CompilerParams(
            dimension_semantics=("arbitrary",), vmem_limit_bytes=VMEM_LIMIT),
        name="proj",
    )(x2d, x2d, x2d, norm_g, w_in_bf, qg, kg, wpool_bf, pscale)


def _mixer_kernel(x_ref, pool_ref, q_ref, k_ref, v_ref, ga_ref, bias_ref, wout_ref, y_ref,
                  attn_ref, *, seq_len):
    j = pl.program_id(1)
    rows = seq_len // GRID_W
    low_half = lax.broadcasted_iota(jnp.int32, (GRID_W, LANES), 1) < HEAD_DIM

    pair_cols = [slice(p * LANES, (p + 1) * LANES) for p in range(N_PAIRS)]

    def window(i):
        r = j * BLOCK_ROWS + i
        rs = jnp.clip(r - WIN_ROWS // 2, 0, rows - WIN_ROWS)
        return r - rs, pl.multiple_of(rs * GRID_W, GRID_W)

    def scores(i):
        _, k0 = window(i)
        out = []
        for c in pair_cols:
            q2 = q_ref[i * GRID_W:(i + 1) * GRID_W, c]
            zero = jnp.zeros_like(q2)
            qq = jnp.concatenate([jnp.where(low_half, q2, zero),
                                  jnp.where(low_half, zero, q2)], axis=0)
            k2 = k_ref[pl.ds(k0, WIN_ROWS * GRID_W), c]
            out.append(lax.dot_general(qq, k2, (((1,), (1,)), ((), ())),
                                       preferred_element_type=F32))
        return out

    def softmax(i, s_list):
        delta, _ = window(i)
        out = []
        for p, s in enumerate(s_list):
            s = s + bias_ref[p, delta]
            e = jnp.exp2(s - jnp.max(s, axis=-1, keepdims=True))
            out.append((e.astype(BF16), 1.0 / jnp.sum(e, axis=-1, keepdims=True)))
        return out

    def weighted_values(i, e_list):
        _, k0 = window(i)
        tok = slice(i * GRID_W, (i + 1) * GRID_W)
        for p, (e, inv_l) in enumerate(e_list):
            c = pair_cols[p]
            v2 = v_ref[pl.ds(k0, WIN_ROWS * GRID_W), c]
            o = jnp.dot(e, v2, preferred_element_type=F32) * inv_l
            o2 = jnp.where(low_half, o[:GRID_W], o[GRID_W:])
            attn_ref[tok, c] = (o2 * ga_ref[tok, c].astype(F32)).astype(BF16)

    s_rows, e_rows = {}, {}
    for t in range(BLOCK_ROWS + 2):
        if t < BLOCK_ROWS:
            s_rows[t] = scores(t)
        if 1 <= t <= BLOCK_ROWS:
            e_rows[t - 1] = softmax(t - 1, s_rows.pop(t - 1))
        if t >= 2:
            weighted_values(t - 2, e_rows.pop(t - 2))

    y = jnp.dot(pool_ref[...], wout_ref[0:POOL_WIDTH, :], preferred_element_type=F32)
    y = y + jnp.dot(attn_ref[...], wout_ref[POOL_WIDTH:, :], preferred_element_type=F32)
    y_ref[...] = x_ref[...] + y


def _mixer_call(x, pool, q, k, v, ga, bias, wout_bf):
    bsz, seq_len, _ = x.shape
    tb = BLOCK_TOKENS
    blk = lambda b, j: (b, j, 0)
    whole = lambda b, j: (b, 0, 0)
    return pl.pallas_call(
        functools.partial(_mixer_kernel, seq_len=seq_len),
        grid=(bsz, seq_len // tb),
        in_specs=[
            pl.BlockSpec((None, tb, D_MODEL), blk),
            pl.BlockSpec((None, tb, POOL_WIDTH), blk),
            pl.BlockSpec((None, tb, ATTN_WIDTH), blk),
            _resident((None, seq_len, ATTN_WIDTH), whole),
            _resident((None, seq_len, ATTN_WIDTH), whole),
            pl.BlockSpec((None, tb, ATTN_WIDTH), blk),
            _resident(bias.shape, lambda b, j: (0, 0, 0, 0)),
            _resident((D_MODEL, D_MODEL), lambda b, j: (0, 0)),
        ],
        out_specs=pl.BlockSpec((None, tb, D_MODEL), blk),
        out_shape=jax.ShapeDtypeStruct(x.shape, F32),
        scratch_shapes=[pltpu.VMEM((tb, ATTN_WIDTH), BF16)],
        compiler_params=pltpu.CompilerParams(
            dimension_semantics=("arbitrary", "arbitrary"), vmem_limit_bytes=VMEM_LIMIT),
        name="mixer",
    )(x, pool, q, k, v, ga, bias, wout_bf)


def _attention_bias(rpb):
    col = np.arange(GRID_W)
    col_start = np.clip(col - WIN_COLS // 2, 0, GRID_W - WIN_COLS)
    kc = np.arange(GRID_W)
    valid = (kc[None, :] >= col_start[:, None]) & (kc[None, :] < col_start[:, None] + WIN_COLS)
    col_off = np.clip(kc[None, :] - col[:, None] + (WIN_COLS - 1), 0, 2 * WIN_COLS - 2)
    table = jnp.where(valid[None, None], rpb[:, :, col_off] * LOG2E, MASKED)
    row_off = np.arange(WIN_ROWS)[None, :] - np.arange(WIN_ROWS)[:, None] + (WIN_ROWS - 1)
    t = table[:, row_off]
    t = jnp.transpose(t, (0, 1, 3, 2, 4)).reshape(N_PAIRS, 2, WIN_ROWS, GRID_W, WIN_ROWS * GRID_W)
    t = jnp.transpose(t, (0, 2, 1, 3, 4))
    return t.reshape(N_PAIRS, WIN_ROWS, 2 * GRID_W, WIN_ROWS * GRID_W).astype(F32)


def _encoder_layer(x, norm_g, w_in, w_pool, pool_scale, q_norm_g, k_norm_g, rpb, w_out):
    bsz, seq_len, _ = x.shape
    assert seq_len % BLOCK_TOKENS == 0 and seq_len // GRID_W >= WIN_ROWS
    assert seq_len % PROJ_TOKENS == 0
    qg = jnp.tile(q_norm_g, N_HEADS)[None, :]
    kg = jnp.tile(k_norm_g, N_HEADS)[None, :]
    pool, q, k, v, ga = _proj_call(x.reshape(bsz * seq_len, D_MODEL), seq_len, norm_g[None, :],
                                   w_in.astype(BF16), qg, kg, w_pool.astype(BF16),
                                   pool_scale[None, :])
    shp = (bsz, seq_len, 512)
    return _mixer_call(x, pool.reshape(shp), q.reshape(shp), k.reshape(shp), v.reshape(shp),
                       ga.reshape(shp), _attention_bias(rpb), w_out.astype(BF16))


def kernel(x_prompt, x_sample, norm_g, w_in, w_pool, pool_scale, q_norm_g, k_norm_g, rpb, w_out):
    y_prompt, y_sample = x_prompt, x_sample
    for l in range(norm_g.shape[0]):
        params = (norm_g[l], w_in[l], w_pool[l], pool_scale[l], q_norm_g[l], k_norm_g[l],
                  rpb[l], w_out[l])
        y_prompt = _encoder_layer(y_prompt, *params)
        y_sample = _encoder_layer(y_sample, *params)
    return (y_prompt, y_sample)
```

```python
import functools
import math

import jax
import jax.numpy as jnp
import numpy as np
from jax import lax
from jax.experimental import pallas as pl
from jax.experimental.pallas import tpu as pltpu

D_MODEL = 1024
POOL_WIDTH = 512
ATTN_WIDTH = 512
POOL_WINDOWS = (2, 4, 8, 16)
POOL_GROUP_DIM = POOL_WIDTH // len(POOL_WINDOWS)
HEAD_DIM = 64
N_HEADS = ATTN_WIDTH // HEAD_DIM
GRID_W = 64
WIN_ROWS = 8
WIN_COLS = 16
IN_WIDTH = 2 * POOL_WIDTH + 4 * ATTN_WIDTH
EPS = 1e-6
LOG2E = math.log2(math.e)

LANES = 128
N_PAIRS = ATTN_WIDTH // LANES
HALO = 16
MASKED = -1e30
COL_BLOCK = 16
N_COL_BLOCKS = GRID_W // COL_BLOCK
_KEY_BLOCKS = {0: (0, 1), 1: (0, 1), 2: (0, 1), 3: (1, 2), 4: (1, 2), 5: (2, 3), 6: (2, 3),
               7: (2, 3)}

PROJ_TOKENS = 1024
BLOCK_ROWS = 8
BLOCK_TOKENS = BLOCK_ROWS * GRID_W
VMEM_LIMIT = 56 * 1024 * 1024

F32 = jnp.float32
BF16 = jnp.bfloat16


def _resident(shape, index_map):
    return pl.BlockSpec(shape, index_map, pipeline_mode=pl.Buffered(1))


def _silu(g):
    return g * (1.0 / (1.0 + jnp.exp(-g)))


def _head_rms_norm(blk, gain, low_half):
    sq = blk * blk
    s_lo = jnp.sum(jnp.where(low_half, sq, 0.0), axis=-1, keepdims=True)
    s_hi = jnp.sum(jnp.where(low_half, 0.0, sq), axis=-1, keepdims=True)
    ms = jnp.where(low_half, s_lo, s_hi) * (1.0 / HEAD_DIM)
    return blk * lax.rsqrt(ms + EPS) * gain


def _proj_kernel(x_ref, xprev_ref, xnext_ref, ng_ref, win_ref, qg_ref, kg_ref, wpool_ref,
                 pscale_ref, pool_ref, q_ref, k_ref, v_ref, ga_ref, uext_ref,
                 *, seq_len):
    tm = x_ref.shape[0]
    tiles_per_seq = seq_len // tm
    pos = pl.program_id(0) % tiles_per_seq

    def normed(x):
        ms = jnp.mean(x * x, axis=-1, keepdims=True)
        return (x * lax.rsqrt(ms + EPS) * ng_ref[...]).astype(BF16)

    h = normed(x_ref[...])

    def seg(i):
        return jnp.dot(h, win_ref[:, i * 512:(i + 1) * 512], preferred_element_type=F32)

    h_ext = jnp.concatenate([normed(xprev_ref[...]), h, normed(xnext_ref[...])], axis=0)
    u_ext = jnp.dot(h_ext, win_ref[:, 0:POOL_WIDTH], preferred_element_type=F32)
    uext_ref[0:HALO, :] = jnp.where(pos > 0, u_ext[0:HALO], 0.0)
    uext_ref[HALO:HALO + tm, :] = u_ext[HALO:HALO + tm]
    uext_ref[HALO + tm:, :] = jnp.where(pos < tiles_per_seq - 1, u_ext[HALO + tm:], 0.0)
    gate = _silu(seg(1))
    t = pos * tm + lax.broadcasted_iota(jnp.int32, (tm, 1), 0)
    pooled = []
    for g, w in enumerate(POOL_WINDOWS):
        a = w // 2
        b = w - a
        c = slice(g * POOL_GROUP_DIM, (g + 1) * POOL_GROUP_DIM)
        acc = uext_ref[HALO - a:HALO - a + tm, c]
        for d in range(-a + 1, b):
            acc = acc + uext_ref[HALO + d:HALO + d + tm, c]
        count = (jnp.minimum(t + b, seq_len) - jnp.maximum(t - a, 0)).astype(F32)
        pooled.append((acc / count - uext_ref[HALO:HALO + tm, c]).astype(BF16))

    low_half = lax.broadcasted_iota(jnp.int32, (tm, LANES), 1) < HEAD_DIM
    q = seg(2)
    for p in range(N_PAIRS):
        c = slice(p * LANES, (p + 1) * LANES)
        qn = _head_rms_norm(q[:, c], qg_ref[:, c], low_half) * (HEAD_DIM ** -0.5 * LOG2E)
        q_ref[:, c] = qn.astype(BF16)
    k = seg(3)
    for p in range(N_PAIRS):
        c = slice(p * LANES, (p + 1) * LANES)
        k_ref[:, c] = _head_rms_norm(k[:, c], kg_ref[:, c], low_half).astype(BF16)
    v_ref[...] = seg(4).astype(BF16)
    ga_ref[...] = _silu(seg(5)).astype(BF16)

    for g in range(len(POOL_WINDOWS)):
        c = slice(g * POOL_GROUP_DIM, (g + 1) * POOL_GROUP_DIM)
        mixed = jnp.dot(pooled[g], wpool_ref[g], preferred_element_type=F32)
        pool_ref[:, c] = (mixed * pscale_ref[:, c] * gate[:, c]).astype(BF16)


def _proj_call(x2d, seq_len, norm_g, w_in_bf, qg, kg, wpool_bf, pscale):
    n = x2d.shape[0]
    tm = PROJ_TOKENS
    halo_blocks = tm // HALO
    n_halo = n // HALO
    out = jax.ShapeDtypeStruct((n, 512), BF16)
    row = lambda i: (i, 0)
    fixed2 = lambda i: (0, 0)
    return pl.pallas_call(
        functools.partial(_proj_kernel, seq_len=seq_len),
        grid=(n // tm,),
        in_specs=[
            pl.BlockSpec((tm, D_MODEL), row),
            pl.BlockSpec((HALO, D_MODEL), lambda i: (jnp.maximum(i * halo_blocks - 1, 0), 0)),
            pl.BlockSpec((HALO, D_MODEL),
                         lambda i: (jnp.minimum((i + 1) * halo_blocks, n_halo - 1), 0)),
            _resident((1, D_MODEL), fixed2),
            _resident((D_MODEL, IN_WIDTH), fixed2),
            _resident((1, ATTN_WIDTH), fixed2),
            _resident((1, ATTN_WIDTH), fixed2),
            _resident(wpool_bf.shape, lambda i: (0, 0, 0)),
            _resident((1, POOL_WIDTH), fixed2),
        ],
        out_specs=[pl.BlockSpec((tm, 512), row)] * 5,
        out_shape=[out] * 5,
        scratch_shapes=[pltpu.VMEM((tm + 2 * HALO, POOL_WIDTH), F32)],
        compiler_params=pltpu.CompilerParams(
            dimension_semantics=("arbitrary",), vmem_limit_bytes=VMEM_LIMIT),
        name="proj",
    )(x2d, x2d, x2d, norm_g, w_in_bf, qg, kg, wpool_bf, pscale)


def _mixer_kernel(x_ref, pool_ref, q_ref, k_ref, v_ref, ga_ref, bias_ref, wout_ref, y_ref,
                  attn_ref, *, seq_len):
    j = pl.program_id(1)
    rows = seq_len // GRID_W
    low_half = lax.broadcasted_iota(jnp.int32, (GRID_W, LANES), 1) < HEAD_DIM
    low16 = lax.broadcasted_iota(jnp.int32, (COL_BLOCK, LANES), 1) < HEAD_DIM

    pair_cols = [slice(p * LANES, (p + 1) * LANES) for p in range(N_PAIRS)]
    nt = (((1,), (1,)), ((), ()))
    zero_blk = jnp.zeros((8, LANES), F32)

    def window(i):
        r = j * BLOCK_ROWS + i
        rs = jnp.clip(r - WIN_ROWS // 2, 0, rows - WIN_ROWS)
        return r - rs, rs * GRID_W

    def key_major(ref, k0, c):
        return jnp.concatenate(
            [ref[pl.ds(pl.multiple_of(k0 + i * GRID_W + n * COL_BLOCK, COL_BLOCK), COL_BLOCK), c]
             for n in range(N_COL_BLOCKS) for i in range(WIN_ROWS)], axis=0)

    def scores(i):
        _, k0 = window(i)
        out = []
        for c in pair_cols:
            q2 = q_ref[i * GRID_W:(i + 1) * GRID_W, c]
            zero = jnp.zeros_like(q2)
            qa = jnp.where(low_half, q2, zero)
            qb = jnp.where(low_half, zero, q2)
            qq = jnp.concatenate(
                [blk[g * COL_BLOCK:(g + 1) * COL_BLOCK] for g in range(N_COL_BLOCKS)
                 for blk in (qa, qb)], axis=0)
            kk = key_major(k_ref, k0, c)
            d0 = lax.dot_general(qq[:96], kk[:256], nt, preferred_element_type=F32)
            d1 = lax.dot_general(qq[32:], kk[256:], nt, preferred_element_type=F32)
            pieces = []
            for m8 in range(2 * N_COL_BLOCKS * 2):
                g, hh, mm = m8 // 4, (m8 // 2) % 2, m8 % 2
                row = []
                for n in _KEY_BLOCKS[2 * g + mm]:
                    src, r0, l0 = (d0, m8 * 8, n * LANES) if n < 2 else (d1, m8 * 8 - 32,
                                                                         (n - 2) * LANES)
                    row.append(src[r0:r0 + 8, l0:l0 + LANES])
                pieces.append(jnp.concatenate(row, axis=1))
            out.append(jnp.concatenate(pieces, axis=0))
        return out

    def softmax(i, s_list):
        delta, _ = window(i)
        out = []
        for p, s in enumerate(s_list):
            s = s + bias_ref[p, delta]
            e = jnp.exp2(s - jnp.max(s, axis=-1, keepdims=True))
            inv_l = 1.0 / jnp.sum(e, axis=-1, keepdims=True)
            halves = []
            for t in range(2):
                rows8 = []
                for m8 in range(4 * t, 4 * t + 12):
                    blocks = _KEY_BLOCKS[2 * (m8 // 4) + m8 % 2]
                    rows8.append(jnp.concatenate(
                        [e[m8 * 8:m8 * 8 + 8, blocks.index(n) * LANES:(blocks.index(n) + 1) * LANES]
                         if n in blocks else zero_blk for n in (2 * t, 2 * t + 1)], axis=1))
                halves.append(jnp.concatenate(rows8, axis=0).astype(BF16))
            out.append((halves, inv_l))
        return out

    def weighted_values(i, e_list):
        _, k0 = window(i)
        tok = slice(i * GRID_W, (i + 1) * GRID_W)
        for p, ((p0, p1), inv_l) in enumerate(e_list):
            c = pair_cols[p]
            vv = key_major(v_ref, k0, c)
            o0 = jnp.dot(p0, vv[:256], preferred_element_type=F32)
            o1 = jnp.dot(p1, vv[256:], preferred_element_type=F32)
            o = jnp.concatenate([o0[:32], o0[32:] + o1[:64], o1[64:]], axis=0) * inv_l
            o2 = jnp.concatenate(
                [jnp.where(low16, o[g * 32:g * 32 + 16], o[g * 32 + 16:g * 32 + 32])
                 for g in range(N_COL_BLOCKS)], axis=0)
            attn_ref[tok, c] = (o2 * ga_ref[tok, c].astype(F32)).astype(BF16)

    s_rows, e_rows = {}, {}
    for t in range(BLOCK_ROWS + 2):
        if t < BLOCK_ROWS:
            s_rows[t] = scores(t)
        if 1 <= t <= BLOCK_ROWS:
            e_rows[t - 1] = softmax(t - 1, s_rows.pop(t - 1))
        if t >= 2:
            weighted_values(t - 2, e_rows.pop(t - 2))

    y = jnp.dot(pool_ref[...], wout_ref[0:POOL_WIDTH, :], preferred_element_type=F32)
    y = y + jnp.dot(attn_ref[...], wout_ref[POOL_WIDTH:, :], preferred_element_type=F32)
    y_ref[...] = x_ref[...] + y


def _mixer_call(x, pool, q, k, v, ga, bias, wout_bf):
    bsz, seq_len, _ = x.shape
    tb = BLOCK_TOKENS
    blk = lambda b, j: (b, j, 0)
    whole = lambda b, j: (b, 0, 0)
    return pl.pallas_call(
        functools.partial(_mixer_kernel, seq_len=seq_len),
        grid=(bsz, seq_len // tb),
        in_specs=[
            pl.BlockSpec((None, tb, D_MODEL), blk),
            pl.BlockSpec((None, tb, POOL_WIDTH), blk),
            pl.BlockSpec((None, tb, ATTN_WIDTH), blk),
            _resident((None, seq_len, ATTN_WIDTH), whole),
            _resident((None, seq_len, ATTN_WIDTH), whole),
            pl.BlockSpec((None, tb, ATTN_WIDTH), blk),
            _resident(bias.shape, lambda b, j: (0, 0, 0, 0)),
            _resident((D_MODEL, D_MODEL), lambda b, j: (0, 0)),
        ],
        out_specs=pl.BlockSpec((None, tb, D_MODEL), blk),
        out_shape=jax.ShapeDtypeStruct(x.shape, F32),
        scratch_shapes=[pltpu.VMEM((tb, ATTN_WIDTH), BF16)],
        compiler_params=pltpu.CompilerParams(
            dimension_semantics=("arbitrary", "arbitrary"), vmem_limit_bytes=VMEM_LIMIT),
        name="mixer",
    )(x, pool, q, k, v, ga, bias, wout_bf)


def _attention_bias(rpb):
    col = np.arange(GRID_W)
    blocks = np.array([_KEY_BLOCKS[m] for m in range(GRID_W // 8)])[col // 8]
    kc = COL_BLOCK * blocks[:, :, None] + np.arange(COL_BLOCK)[None, None, :]
    col_start = np.clip(col - WIN_COLS // 2, 0, GRID_W - WIN_COLS)[:, None, None]
    valid = (kc >= col_start) & (kc < col_start + WIN_COLS)
    col_off = np.clip(kc - col[:, None, None] + (WIN_COLS - 1), 0, 2 * WIN_COLS - 2)
    table = jnp.where(valid[None, None], rpb[:, :, col_off] * LOG2E, MASKED)
    row_off = np.arange(WIN_ROWS)[None, :] - np.arange(WIN_ROWS)[:, None] + (WIN_ROWS - 1)
    t = table[:, row_off]
    t = t.reshape(N_PAIRS, 2, WIN_ROWS, WIN_ROWS, N_COL_BLOCKS, COL_BLOCK, 2, COL_BLOCK)
    t = jnp.transpose(t, (0, 2, 4, 1, 5, 6, 3, 7))
    return t.reshape(N_PAIRS, WIN_ROWS, 2 * GRID_W, 2 * LANES).astype(F32)


def _encoder_layer(x, norm_g, w_in, w_pool, pool_scale, q_norm_g, k_norm_g, rpb, w_out):
    bsz, seq_len, _ = x.shape
    assert seq_len % BLOCK_TOKENS == 0 and seq_len // GRID_W >= WIN_ROWS
    assert seq_len % PROJ_TOKENS == 0
    qg = jnp.tile(q_norm_g, N_HEADS)[None, :]
    kg = jnp.tile(k_norm_g, N_HEADS)[None, :]
    pool, q, k, v, ga = _proj_call(x.reshape(bsz * seq_len, D_MODEL), seq_len, norm_g[None, :],
                                   w_in.astype(BF16), qg, kg, w_pool.astype(BF16),
                                   pool_scale[None, :])
    shp = (bsz, seq_len, 512)
    return _mixer_call(x, pool.reshape(shp), q.reshape(shp), k.reshape(shp), v.reshape(shp),
                       ga.reshape(shp), _attention_bias(rpb), w_out.astype(BF16))


def kernel(x_prompt, x_sample, norm_g, w_in, w_pool, pool_scale, q_norm_g, k_norm_g, rpb, w_out):
    y_prompt, y_sample = x_prompt, x_sample
    for l in range(norm_g.shape[0]):
        params = (norm_g[l], w_in[l], w_pool[l], pool_scale[l], q_norm_g[l], k_norm_g[l],
                  rpb[l], w_out[l])
        y_prompt = _encoder_layer(y_prompt, *params)
        y_sample = _encoder_layer(y_sample, *params)
    return (y_prompt, y_sample)
```

```python
import functools
import math

import jax
import jax.numpy as jnp
import numpy as np
from jax import lax
from jax.experimental import pallas as pl
from jax.experimental.pallas import tpu as pltpu

D_MODEL = 1024
POOL_WIDTH = 512
ATTN_WIDTH = 512
POOL_WINDOWS = (2, 4, 8, 16)
POOL_GROUP_DIM = POOL_WIDTH // len(POOL_WINDOWS)
HEAD_DIM = 64
N_HEADS = ATTN_WIDTH // HEAD_DIM
GRID_W = 64
WIN_ROWS = 8
WIN_COLS = 16
IN_WIDTH = 2 * POOL_WIDTH + 4 * ATTN_WIDTH
EPS = 1e-6
LOG2E = math.log2(math.e)

LANES = 128
N_PAIRS = ATTN_WIDTH // LANES
HALO = 16
MASKED = -1e30
COL_BLOCK = 16
N_COL_BLOCKS = GRID_W // COL_BLOCK
_KEY_BLOCKS = {0: (0, 1), 1: (0, 1), 2: (0, 1), 3: (1, 2), 4: (1, 2), 5: (2, 3), 6: (2, 3),
               7: (2, 3)}

PROJ_TOKENS = 1024
BLOCK_ROWS = 8
BLOCK_TOKENS = BLOCK_ROWS * GRID_W
VMEM_LIMIT = 56 * 1024 * 1024

F32 = jnp.float32
BF16 = jnp.bfloat16


def _resident(shape, index_map):
    return pl.BlockSpec(shape, index_map, pipeline_mode=pl.Buffered(1))


def _silu(g):
    return g * (1.0 / (1.0 + jnp.exp(-g)))


def _head_rms_norm(blk, gain, low_half):
    sq = blk * blk
    s_lo = jnp.sum(jnp.where(low_half, sq, 0.0), axis=-1, keepdims=True)
    s_hi = jnp.sum(jnp.where(low_half, 0.0, sq), axis=-1, keepdims=True)
    ms = jnp.where(low_half, s_lo, s_hi) * (1.0 / HEAD_DIM)
    return blk * lax.rsqrt(ms + EPS) * gain


def _proj_kernel(x_ref, xprev_ref, xnext_ref, ng_ref, win_ref, qg_ref, kg_ref, wpool_ref,
                 pscale_ref, pool_ref, q_ref, k_ref, v_ref, ga_ref, uext_ref,
                 *, seq_len):
    tm = x_ref.shape[0]
    tiles_per_seq = seq_len // tm
    pos = pl.program_id(0) % tiles_per_seq

    def normed(x):
        ms = jnp.mean(x * x, axis=-1, keepdims=True)
        return (x * lax.rsqrt(ms + EPS) * ng_ref[...]).astype(BF16)

    h = normed(x_ref[...])

    def seg(i):
        return jnp.dot(h, win_ref[:, i * 512:(i + 1) * 512], preferred_element_type=F32)

    h_ext = jnp.concatenate([normed(xprev_ref[...]), h, normed(xnext_ref[...])], axis=0)
    u_ext = jnp.dot(h_ext, win_ref[:, 0:POOL_WIDTH], preferred_element_type=F32)
    uext_ref[0:HALO, :] = jnp.where(pos > 0, u_ext[0:HALO], 0.0)
    uext_ref[HALO:HALO + tm, :] = u_ext[HALO:HALO + tm]
    uext_ref[HALO + tm:, :] = jnp.where(pos < tiles_per_seq - 1, u_ext[HALO + tm:], 0.0)
    gate = _silu(seg(1))
    t = pos * tm + lax.broadcasted_iota(jnp.int32, (tm, 1), 0)
    pooled = []
    for g, w in enumerate(POOL_WINDOWS):
        a = w // 2
        b = w - a
        c = slice(g * POOL_GROUP_DIM, (g + 1) * POOL_GROUP_DIM)
        acc = uext_ref[HALO - a:HALO - a + tm, c]
        for d in range(-a + 1, b):
            acc = acc + uext_ref[HALO + d:HALO + d + tm, c]
        count = (jnp.minimum(t + b, seq_len) - jnp.maximum(t - a, 0)).astype(F32)
        pooled.append((acc / count - uext_ref[HALO:HALO + tm, c]).astype(BF16))

    low_half = lax.broadcasted_iota(jnp.int32, (tm, LANES), 1) < HEAD_DIM
    q = seg(2)
    for p in range(N_PAIRS):
        c = slice(p * LANES, (p + 1) * LANES)
        qn = _head_rms_norm(q[:, c], qg_ref[:, c], low_half) * (HEAD_DIM ** -0.5 * LOG2E)
        q_ref[:, c] = qn.astype(BF16)
    k = seg(3)
    for p in range(N_PAIRS):
        c = slice(p * LANES, (p + 1) * LANES)
        k_ref[:, c] = _head_rms_norm(k[:, c], kg_ref[:, c], low_half).astype(BF16)
    v_ref[...] = seg(4).astype(BF16)
    ga_ref[...] = _silu(seg(5)).astype(BF16)

    for g in range(len(POOL_WINDOWS)):
        c = slice(g * POOL_GROUP_DIM, (g + 1) * POOL_GROUP_DIM)
        mixed = jnp.dot(pooled[g], wpool_ref[g], preferred_element_type=F32)
        pool_ref[:, c] = (mixed * pscale_ref[:, c] * gate[:, c]).astype(BF16)


def _proj_call(x2d, seq_len, norm_g, w_in_bf, qg, kg, wpool_bf, pscale):
    n = x2d.shape[0]
    tm = PROJ_TOKENS
    halo_blocks = tm // HALO
    n_halo = n // HALO
    out = jax.ShapeDtypeStruct((n, 512), BF16)
    row = lambda i: (i, 0)
    fixed2 = lambda i: (0, 0)
    return pl.pallas_call(
        functools.partial(_proj_kernel, seq_len=seq_len),
        grid=(n // tm,),
        in_specs=[
            pl.BlockSpec((tm, D_MODEL), row),
            pl.BlockSpec((HALO, D_MODEL), lambda i: (jnp.maximum(i * halo_blocks - 1, 0), 0)),
            pl.BlockSpec((HALO, D_MODEL),
                         lambda i: (jnp.minimum((i + 1) * halo_blocks, n_halo - 1), 0)),
            _resident((1, D_MODEL), fixed2),
            _resident((D_MODEL, IN_WIDTH), fixed2),
            _resident((1, ATTN_WIDTH), fixed2),
            _resident((1, ATTN_WIDTH), fixed2),
            _resident(wpool_bf.shape, lambda i: (0, 0, 0)),
            _resident((1, POOL_WIDTH), fixed2),
        ],
        out_specs=[pl.BlockSpec((tm, 512), row)] * 5,
        out_shape=[out] * 5,
        scratch_shapes=[pltpu.VMEM((tm + 2 * HALO, POOL_WIDTH), F32)],
        compiler_params=pltpu.CompilerParams(
            dimension_semantics=("arbitrary",), vmem_limit_bytes=VMEM_LIMIT),
        name="proj",
    )(x2d, x2d, x2d, norm_g, w_in_bf, qg, kg, wpool_bf, pscale)


def _mixer_kernel(x_ref, pool_ref, q_ref, k_ref, v_ref, ga_ref, bias_ref, wout_ref, y_ref,
                  attn_ref, *, seq_len):
    j = pl.program_id(1)
    rows = seq_len // GRID_W
    low_half = lax.broadcasted_iota(jnp.int32, (GRID_W, LANES), 1) < HEAD_DIM
    low16 = lax.broadcasted_iota(jnp.int32, (COL_BLOCK, LANES), 1) < HEAD_DIM

    pair_cols = [slice(p * LANES, (p + 1) * LANES) for p in range(N_PAIRS)]
    nt = (((1,), (1,)), ((), ()))
    zero_blk = jnp.zeros((8, LANES), F32)

    def window(i):
        r = j * BLOCK_ROWS + i
        rs = jnp.clip(r - WIN_ROWS // 2, 0, rows - WIN_ROWS)
        return r - rs, rs * GRID_W

    def key_major(ref, k0, c):
        return jnp.concatenate(
            [ref[pl.ds(pl.multiple_of(k0 + i * GRID_W + n * COL_BLOCK, COL_BLOCK), COL_BLOCK), c]
             for n in range(N_COL_BLOCKS) for i in range(WIN_ROWS)], axis=0)

    def scores(i):
        _, k0 = window(i)
        out = []
        for c in pair_cols:
            q2 = q_ref[i * GRID_W:(i + 1) * GRID_W, c]
            zero = jnp.zeros_like(q2)
            qa = jnp.where(low_half, q2, zero)
            qb = jnp.where(low_half, zero, q2)
            qq = jnp.concatenate(
                [blk[g * COL_BLOCK:(g + 1) * COL_BLOCK] for g in range(N_COL_BLOCKS)
                 for blk in (qa, qb)], axis=0)
            kk = key_major(k_ref, k0, c)
            d0 = lax.dot_general(qq[:96], kk[:256], nt, preferred_element_type=F32)
            d1 = lax.dot_general(qq[32:], kk[256:], nt, preferred_element_type=F32)
            pieces = []
            for m8 in range(2 * N_COL_BLOCKS * 2):
                g, hh, mm = m8 // 4, (m8 // 2) % 2, m8 % 2
                row = []
                for n in _KEY_BLOCKS[2 * g + mm]:
                    src, r0, l0 = (d0, m8 * 8, n * LANES) if n < 2 else (d1, m8 * 8 - 32,
                                                                         (n - 2) * LANES)
                    row.append(src[r0:r0 + 8, l0:l0 + LANES])
                pieces.append(jnp.concatenate(row, axis=1))
            out.append(jnp.concatenate(pieces, axis=0))
        return out

    def softmax(i, s_list):
        delta, _ = window(i)
        out = []
        for p, s in enumerate(s_list):
            s = s + bias_ref[p, delta]
            e = jnp.exp2(s - jnp.max(s, axis=-1, keepdims=True))
            inv_l = 1.0 / jnp.sum(e, axis=-1, keepdims=True)
            halves = []
            for t in range(2):
                rows8 = []
                for m8 in range(4 * t, 4 * t + 12):
                    blocks = _KEY_BLOCKS[2 * (m8 // 4) + m8 % 2]
                    rows8.append(jnp.concatenate(
                        [e[m8 * 8:m8 * 8 + 8, blocks.index(n) * LANES:(blocks.index(n) + 1) * LANES]
                         if n in blocks else zero_blk for n in (2 * t, 2 * t + 1)], axis=1))
                halves.append(jnp.concatenate(rows8, axis=0).astype(BF16))
            out.append((halves, inv_l))
        return out

    def weighted_values(i, e_list):
        _, k0 = window(i)
        tok = slice(i * GRID_W, (i + 1) * GRID_W)
        for p, ((p0, p1), inv_l) in enumerate(e_list):
            c = pair_cols[p]
            vv = key_major(v_ref, k0, c)
            o0 = jnp.dot(p0, vv[:256], preferred_element_type=F32)
            o1 = jnp.dot(p1, vv[256:], preferred_element_type=F32)
            o = jnp.concatenate([o0[:32], o0[32:] + o1[:64], o1[64:]], axis=0) * inv_l
            o2 = jnp.concatenate(
                [jnp.where(low16, o[g * 32:g * 32 + 16], o[g * 32 + 16:g * 32 + 32])
                 for g in range(N_COL_BLOCKS)], axis=0)
            attn_ref[tok, c] = (o2 * ga_ref[tok, c].astype(F32)).astype(BF16)

    s_rows, e_rows = {}, {}
    for t in range(BLOCK_ROWS + 2):
        if t < BLOCK_ROWS:
            s_rows[t] = scores(t)
        if 1 <= t <= BLOCK_ROWS:
            e_rows[t - 1] = softmax(t - 1, s_rows.pop(t - 1))
        if t >= 2:
            weighted_values(t - 2, e_rows.pop(t - 2))

    y = jnp.dot(pool_ref[...], wout_ref[0:POOL_WIDTH, :], preferred_element_type=F32)
    y = y + jnp.dot(attn_ref[...], wout_ref[POOL_WIDTH:, :], preferred_element_type=F32)
    y_ref[...] = x_ref[...] + y


def _mixer_call(x, pool, q, k, v, ga, bias, wout_bf):
    bsz, seq_len, _ = x.shape
    tb = BLOCK_TOKENS
    blk = lambda b, j: (b, j, 0)
    whole = lambda b, j: (b, 0, 0)
    return pl.pallas_call(
        functools.partial(_mixer_kernel, seq_len=seq_len),
        grid=(bsz, seq_len // tb),
        in_specs=[
            pl.BlockSpec((None, tb, D_MODEL), blk),
            pl.BlockSpec((None, tb, POOL_WIDTH), blk),
            pl.BlockSpec((None, tb, ATTN_WIDTH), blk),
            _resident((None, seq_len, ATTN_WIDTH), whole),
            _resident((None, seq_len, ATTN_WIDTH), whole),
            pl.BlockSpec((None, tb, ATTN_WIDTH), blk),
            _resident(bias.shape, lambda b, j: (0, 0, 0, 0)),
            _resident((D_MODEL, D_MODEL), lambda b, j: (0, 0)),
        ],
        out_specs=pl.BlockSpec((None, tb, D_MODEL), blk),
        out_shape=jax.ShapeDtypeStruct(x.shape, F32),
        scratch_shapes=[pltpu.VMEM((tb, ATTN_WIDTH), BF16)],
        compiler_params=pltpu.CompilerParams(
            dimension_semantics=("arbitrary", "arbitrary"), vmem_limit_bytes=VMEM_LIMIT),
        name="mixer",
    )(x, pool, q, k, v, ga, bias, wout_bf)


def _attention_bias(rpb):
    col = np.arange(GRID_W)
    blocks = np.array([_KEY_BLOCKS[m] for m in range(GRID_W // 8)])[col // 8]
    kc = COL_BLOCK * blocks[:, :, None] + np.arange(COL_BLOCK)[None, None, :]
    col_start = np.clip(col - WIN_COLS // 2, 0, GRID_W - WIN_COLS)[:, None, None]
    valid = ((kc >= col_start) & (kc < col_start + WIN_COLS)).astype(np.float32)
    valid = np.broadcast_to(valid[:, :, None, :], (GRID_W, 2, WIN_ROWS, COL_BLOCK))
    valid = valid.reshape(N_COL_BLOCKS, 1, COL_BLOCK, 2 * LANES)
    valid = np.broadcast_to(valid, (N_COL_BLOCKS, 2, COL_BLOCK, 2 * LANES)).reshape(2 * GRID_W,
                                                                                    2 * LANES)
    rpb_pad = jnp.pad(rpb, ((0, 0), (0, 0), (0, LANES - rpb.shape[-1])))
    return pl.pallas_call(
        _bias_kernel,
        out_shape=jax.ShapeDtypeStruct((N_PAIRS, WIN_ROWS, 2 * GRID_W, 2 * LANES), F32),
        name="bias_table",
    )(rpb_pad, jnp.asarray(valid))


def _bias_kernel(rpb_ref, valid_ref, out_ref):
    lane = lax.broadcasted_iota(jnp.int32, (8, LANES), 1)
    valid = valid_ref[...] > 0.0

    def one_table(idx, carry):
        p, delta = idx // WIN_ROWS, idx % WIN_ROWS
        acc = {}
        for hh in range(2):
            for i in range(WIN_ROWS):
                row = rpb_ref[2 * p + hh, pl.ds(i - delta + (WIN_ROWS - 1), 1), :]
                xb = jnp.broadcast_to(row, (8, LANES))
                in_row_i = (lane >= i * COL_BLOCK) & (lane < (i + 1) * COL_BLOCK)
                for m in range(GRID_W // 8):
                    for half, n in enumerate(_KEY_BLOCKS[m]):
                        shift = (COL_BLOCK * (i - n) + 8 * m - (WIN_COLS - 1)) % LANES
                        rolled = pltpu.roll(xb, shift, 1, stride=1, stride_axis=0)
                        key = (m, hh, half)
                        acc[key] = rolled if i == 0 else jnp.where(in_row_i, rolled, acc[key])
        tile = jnp.concatenate(
            [jnp.concatenate([acc[(2 * g + mm, hh, 0)], acc[(2 * g + mm, hh, 1)]], axis=1)
             for g in range(N_COL_BLOCKS) for hh in range(2) for mm in range(2)], axis=0)
        out_ref[p, delta] = jnp.where(valid, tile * LOG2E, MASKED)
        return carry

    lax.fori_loop(0, N_PAIRS * WIN_ROWS, one_table, 0)


def _encoder_layer(x, norm_g, w_in, w_pool, pool_scale, q_norm_g, k_norm_g, rpb, w_out):
    bsz, seq_len, _ = x.shape
    assert seq_len % BLOCK_TOKENS == 0 and seq_len // GRID_W >= WIN_ROWS
    assert seq_len % PROJ_TOKENS == 0
    qg = jnp.tile(q_norm_g, N_HEADS)[None, :]
    kg = jnp.tile(k_norm_g, N_HEADS)[None, :]
    pool, q, k, v, ga = _proj_call(x.reshape(bsz * seq_len, D_MODEL), seq_len, norm_g[None, :],
                                   w_in.astype(BF16), qg, kg, w_pool.astype(BF16),
                                   pool_scale[None, :])
    shp = (bsz, seq_len, 512)
    return _mixer_call(x, pool.reshape(shp), q.reshape(shp), k.reshape(shp), v.reshape(shp),
                       ga.reshape(shp), _attention_bias(rpb), w_out.astype(BF16))


def kernel(x_prompt, x_sample, norm_g, w_in, w_pool, pool_scale, q_norm_g, k_norm_g, rpb, w_out):
    y_prompt, y_sample = x_prompt, x_sample
    for l in range(norm_g.shape[0]):
        params = (norm_g[l], w_in[l], w_pool[l], pool_scale[l], q_norm_g[l], k_norm_g[l],
                  rpb[l], w_out[l])
        y_prompt = _encoder_layer(y_prompt, *params)
        y_sample = _encoder_layer(y_sample, *params)
    return (y_prompt, y_sample)
```

```python
import functools
import math

import jax
import jax.numpy as jnp
import numpy as np
from jax import lax
from jax.experimental import pallas as pl
from jax.experimental.pallas import tpu as pltpu

D_MODEL = 1024
POOL_WIDTH = 512
ATTN_WIDTH = 512
POOL_WINDOWS = (2, 4, 8, 16)
POOL_GROUP_DIM = POOL_WIDTH // len(POOL_WINDOWS)
HEAD_DIM = 64
N_HEADS = ATTN_WIDTH // HEAD_DIM
GRID_W = 64
WIN_ROWS = 8
WIN_COLS = 16
IN_WIDTH = 2 * POOL_WIDTH + 4 * ATTN_WIDTH
EPS = 1e-6
LOG2E = math.log2(math.e)

LANES = 128
N_PAIRS = ATTN_WIDTH // LANES
HALO = 16
MASKED = -1e30
COL_BLOCK = 16
N_COL_BLOCKS = GRID_W // COL_BLOCK
_KEY_BLOCKS = {0: (0, 1), 1: (0, 1), 2: (0, 1), 3: (1, 2), 4: (1, 2), 5: (2, 3), 6: (2, 3),
               7: (2, 3)}

PROJ_TOKENS = 1024
BLOCK_ROWS = 16
BLOCK_TOKENS = BLOCK_ROWS * GRID_W
KV_ROWS = BLOCK_ROWS + WIN_ROWS
PROJ_CHUNK_ROWS = 4
VMEM_LIMIT = 56 * 1024 * 1024

F32 = jnp.float32
BF16 = jnp.bfloat16


def _resident(shape, index_map):
    return pl.BlockSpec(shape, index_map, pipeline_mode=pl.Buffered(1))


def _silu(g):
    return g * (1.0 / (1.0 + jnp.exp(-g)))


def _head_rms_norm(blk, gain, low_half):
    sq = blk * blk
    s_lo = jnp.sum(jnp.where(low_half, sq, 0.0), axis=-1, keepdims=True)
    s_hi = jnp.sum(jnp.where(low_half, 0.0, sq), axis=-1, keepdims=True)
    ms = jnp.where(low_half, s_lo, s_hi) * (1.0 / HEAD_DIM)
    return blk * lax.rsqrt(ms + EPS) * gain


def _proj_kernel(x_ref, xprev_ref, xnext_ref, ng_ref, win_ref, qg_ref, kg_ref, wpool_ref,
                 pscale_ref, pool_ref, q_ref, k_ref, v_ref, ga_ref, uext_ref, s1_ref, s2_ref,
                 *, seq_len):
    assert POOL_WINDOWS == (2, 4, 8, 16)
    tm = x_ref.shape[0]
    tiles_per_seq = seq_len // tm
    pos = pl.program_id(0) % tiles_per_seq

    def normed(x):
        ms = jnp.mean(x * x, axis=-1, keepdims=True)
        return (x * lax.rsqrt(ms + EPS) * ng_ref[...]).astype(BF16)

    h = normed(x_ref[...])

    def seg(i):
        return jnp.dot(h, win_ref[:, i * 512:(i + 1) * 512], preferred_element_type=F32)

    h_ext = jnp.concatenate([normed(xprev_ref[...]), h, normed(xnext_ref[...])], axis=0)
    u_ext = jnp.dot(h_ext, win_ref[:, 0:POOL_WIDTH], preferred_element_type=F32)
    uext_ref[0:HALO, :] = jnp.where(pos > 0, u_ext[0:HALO], 0.0)
    uext_ref[HALO:HALO + tm, :] = u_ext[HALO:HALO + tm]
    n_ext = tm + 2 * HALO
    uext_ref[HALO + tm:n_ext, :] = jnp.where(pos < tiles_per_seq - 1, u_ext[HALO + tm:], 0.0)
    uext_ref[n_ext:, :] = jnp.zeros((8, POOL_WIDTH), F32)
    gate = _silu(seg(1))

    gd = POOL_GROUP_DIM
    s1_ref[0:n_ext, gd:] = uext_ref[0:n_ext, gd:] + uext_ref[1:n_ext + 1, gd:]
    s2_ref[0:n_ext - 8, 2 * gd:] = s1_ref[0:n_ext - 8, 2 * gd:] + s1_ref[2:n_ext - 6, 2 * gd:]
    s1_ref[0:n_ext - 16, 3 * gd:] = s2_ref[0:n_ext - 16, 3 * gd:] + s2_ref[4:n_ext - 12, 3 * gd:]
    level_refs = (uext_ref, s1_ref, s2_ref, s1_ref)
    pooled = []
    for g, w in enumerate(POOL_WINDOWS):
        a = w // 2
        b = w - a
        c = slice(g * gd, (g + 1) * gd)
        half_sums = level_refs[g]
        acc = half_sums[HALO - a:HALO - a + tm, c] + half_sums[HALO:HALO + tm, c]

        def edge_mean(first):
            t = pos * tm + first + lax.broadcasted_iota(jnp.int32, (HALO, 1), 0)
            count = (jnp.minimum(t + b, seq_len) - jnp.maximum(t - a, 0)).astype(F32)
            return acc[first:first + HALO] / count

        mean = jnp.concatenate([edge_mean(0), acc[HALO:tm - HALO] * (1.0 / w),
                                edge_mean(tm - HALO)], axis=0)
        pooled.append((mean - uext_ref[HALO:HALO + tm, c]).astype(BF16))

    low_half = lax.broadcasted_iota(jnp.int32, (tm, LANES), 1) < HEAD_DIM
    q = seg(2)
    for p in range(N_PAIRS):
        c = slice(p * LANES, (p + 1) * LANES)
        qn = _head_rms_norm(q[:, c], qg_ref[:, c], low_half) * (HEAD_DIM ** -0.5 * LOG2E)
        q_ref[:, c] = qn.astype(BF16)
    k = seg(3)
    for p in range(N_PAIRS):
        c = slice(p * LANES, (p + 1) * LANES)
        k_ref[:, c] = _head_rms_norm(k[:, c], kg_ref[:, c], low_half).astype(BF16)
    v_ref[...] = seg(4).astype(BF16)
    ga_ref[...] = _silu(seg(5)).astype(BF16)

    for g in range(len(POOL_WINDOWS)):
        c = slice(g * POOL_GROUP_DIM, (g + 1) * POOL_GROUP_DIM)
        mixed = jnp.dot(pooled[g], wpool_ref[g], preferred_element_type=F32)
        pool_ref[:, c] = (mixed * pscale_ref[:, c] * gate[:, c]).astype(BF16)


def _proj_call(x2d, seq_len, norm_g, w_in_bf, qg, kg, wpool_bf, pscale):
    n = x2d.shape[0]
    tm = PROJ_TOKENS
    halo_blocks = tm // HALO
    n_halo = n // HALO
    out = jax.ShapeDtypeStruct((n, 512), BF16)
    row = lambda i: (i, 0)
    fixed2 = lambda i: (0, 0)
    return pl.pallas_call(
        functools.partial(_proj_kernel, seq_len=seq_len),
        grid=(n // tm,),
        in_specs=[
            pl.BlockSpec((tm, D_MODEL), row),
            pl.BlockSpec((HALO, D_MODEL), lambda i: (jnp.maximum(i * halo_blocks - 1, 0), 0)),
            pl.BlockSpec((HALO, D_MODEL),
                         lambda i: (jnp.minimum((i + 1) * halo_blocks, n_halo - 1), 0)),
            _resident((1, D_MODEL), fixed2),
            _resident((D_MODEL, IN_WIDTH), fixed2),
            _resident((1, ATTN_WIDTH), fixed2),
            _resident((1, ATTN_WIDTH), fixed2),
            _resident(wpool_bf.shape, lambda i: (0, 0, 0)),
            _resident((1, POOL_WIDTH), fixed2),
        ],
        out_specs=[pl.BlockSpec((tm, 512), row)] * 5,
        out_shape=[out] * 5,
        scratch_shapes=[pltpu.VMEM((tm + 2 * HALO + 8, POOL_WIDTH), F32)] * 3,
        compiler_params=pltpu.CompilerParams(
            dimension_semantics=("arbitrary",), vmem_limit_bytes=VMEM_LIMIT),
        name="proj",
    )(x2d, x2d, x2d, norm_g, w_in_bf, qg, kg, wpool_bf, pscale)


def _mixer_kernel(x_ref, pool_ref, q_ref, k_ref, v_ref, ga_ref, bias_ref, wout_ref, y_ref,
                  attn_ref, *, seq_len):
    j = pl.program_id(1)
    rows = seq_len // GRID_W
    low_half = lax.broadcasted_iota(jnp.int32, (GRID_W, LANES), 1) < HEAD_DIM
    low16 = lax.broadcasted_iota(jnp.int32, (COL_BLOCK, LANES), 1) < HEAD_DIM

    pair_cols = [slice(p * LANES, (p + 1) * LANES) for p in range(N_PAIRS)]
    nt = (((1,), (1,)), ((), ()))
    zero_blk = jnp.zeros((8, LANES), F32)

    kv_first_row = _kv_first_row(j, rows)

    def window(i):
        r = j * BLOCK_ROWS + i
        rs = jnp.clip(r - WIN_ROWS // 2, 0, rows - WIN_ROWS)
        return r - rs, (rs - kv_first_row) * GRID_W

    def key_major(ref, k0, c):
        return jnp.concatenate(
            [ref[0, pl.ds(pl.multiple_of(k0 + i * GRID_W + n * COL_BLOCK, COL_BLOCK), COL_BLOCK), c]
             for n in range(N_COL_BLOCKS) for i in range(WIN_ROWS)], axis=0)

    def scores(i):
        _, k0 = window(i)
        out = []
        for c in pair_cols:
            q2 = q_ref[i * GRID_W:(i + 1) * GRID_W, c]
            zero = jnp.zeros_like(q2)
            qa = jnp.where(low_half, q2, zero)
            qb = jnp.where(low_half, zero, q2)
            qq = jnp.concatenate(
                [blk[g * COL_BLOCK:(g + 1) * COL_BLOCK] for g in range(N_COL_BLOCKS)
                 for blk in (qa, qb)], axis=0)
            kk = key_major(k_ref, k0, c)
            d0 = lax.dot_general(qq[:96], kk[:256], nt, preferred_element_type=F32)
            d1 = lax.dot_general(qq[32:], kk[256:], nt, preferred_element_type=F32)
            pieces = []
            for m8 in range(2 * N_COL_BLOCKS * 2):
                g, hh, mm = m8 // 4, (m8 // 2) % 2, m8 % 2
                row = []
                for n in _KEY_BLOCKS[2 * g + mm]:
                    src, r0, l0 = (d0, m8 * 8, n * LANES) if n < 2 else (d1, m8 * 8 - 32,
                                                                         (n - 2) * LANES)
                    row.append(src[r0:r0 + 8, l0:l0 + LANES])
                pieces.append(jnp.concatenate(row, axis=1))
            out.append(jnp.concatenate(pieces, axis=0))
        return out

    def softmax(i, s_list):
        delta, _ = window(i)
        out = []
        for p, s in enumerate(s_list):
            s = s + bias_ref[p, delta]
            e = jnp.exp2(s - jnp.max(s, axis=-1, keepdims=True))
            inv_l = 1.0 / jnp.sum(e, axis=-1, keepdims=True)
            halves = []
            for t in range(2):
                rows8 = []
                for m8 in range(4 * t, 4 * t + 12):
                    blocks = _KEY_BLOCKS[2 * (m8 // 4) + m8 % 2]
                    rows8.append(jnp.concatenate(
                        [e[m8 * 8:m8 * 8 + 8, blocks.index(n) * LANES:(blocks.index(n) + 1) * LANES]
                         if n in blocks else zero_blk for n in (2 * t, 2 * t + 1)], axis=1))
                halves.append(jnp.concatenate(rows8, axis=0).astype(BF16))
            out.append((halves, inv_l))
        return out

    def weighted_values(i, e_list):
        _, k0 = window(i)
        tok = slice(i * GRID_W, (i + 1) * GRID_W)
        for p, ((p0, p1), inv_l) in enumerate(e_list):
            c = pair_cols[p]
            vv = key_major(v_ref, k0, c)
            o0 = jnp.dot(p0, vv[:256], preferred_element_type=F32)
            o1 = jnp.dot(p1, vv[256:], preferred_element_type=F32)
            o = jnp.concatenate([o0[:32], o0[32:] + o1[:64], o1[64:]], axis=0) * inv_l
            o2 = jnp.concatenate(
                [jnp.where(low16, o[g * 32:g * 32 + 16], o[g * 32 + 16:g * 32 + 32])
                 for g in range(N_COL_BLOCKS)], axis=0)
            attn_ref[tok, c] = (o2 * ga_ref[tok, c].astype(F32)).astype(BF16)

    chunk_tokens = PROJ_CHUNK_ROWS * GRID_W
    n_col_chunks = D_MODEL // 512

    def proj_chunk(idx, src_ref, w_rows, first):
        tok = slice((idx // n_col_chunks) * chunk_tokens, (idx // n_col_chunks + 1) * chunk_tokens)
        cols = slice((idx % n_col_chunks) * 512, (idx % n_col_chunks + 1) * 512)
        upd = jnp.dot(src_ref[tok, :], wout_ref[w_rows, cols], preferred_element_type=F32)
        if first:
            y_ref[tok, cols] = x_ref[tok, cols] + upd
        else:
            y_ref[tok, cols] += upd

    n_chunks = (BLOCK_ROWS // PROJ_CHUNK_ROWS) * n_col_chunks
    attn_stage = [(idx // n_col_chunks + 1) * PROJ_CHUNK_ROWS + 2 + idx % n_col_chunks
                  for idx in range(n_chunks)]
    s_rows, e_rows = {}, {}
    for t in range(BLOCK_ROWS + 2):
        if t < BLOCK_ROWS:
            s_rows[t] = scores(t)
        if 1 <= t and (t - 1) % 2 == 0 and (t - 1) // 2 < n_chunks:
            proj_chunk((t - 1) // 2, pool_ref, slice(0, POOL_WIDTH), True)
        if 1 <= t <= BLOCK_ROWS:
            e_rows[t - 1] = softmax(t - 1, s_rows.pop(t - 1))
        if t >= 2:
            weighted_values(t - 2, e_rows.pop(t - 2))
        for idx in range(n_chunks):
            if attn_stage[idx] == t:
                proj_chunk(idx, attn_ref, slice(POOL_WIDTH, D_MODEL), False)
    for idx in range(n_chunks):
        if attn_stage[idx] > BLOCK_ROWS + 1:
            proj_chunk(idx, attn_ref, slice(POOL_WIDTH, D_MODEL), False)


def _kv_first_row(j, rows):
    return jnp.clip(j * BLOCK_ROWS - WIN_ROWS // 2, 0, rows - KV_ROWS)


def _mixer_call(x, pool, q, k, v, ga, bias, wout_bf):
    bsz, seq_len, _ = x.shape
    tb = BLOCK_TOKENS
    rows = seq_len // GRID_W
    blk = lambda b, j: (b, j, 0)
    kv_spec = pl.BlockSpec(
        (pl.Element(1), pl.Element(KV_ROWS * GRID_W), pl.Element(ATTN_WIDTH)),
        lambda b, j: (b, _kv_first_row(j, rows) * GRID_W, 0))
    return pl.pallas_call(
        functools.partial(_mixer_kernel, seq_len=seq_len),
        grid=(bsz, seq_len // tb),
        in_specs=[
            pl.BlockSpec((None, tb, D_MODEL), blk),
            pl.BlockSpec((None, tb, POOL_WIDTH), blk),
            pl.BlockSpec((None, tb, ATTN_WIDTH), blk),
            kv_spec,
            kv_spec,
            pl.BlockSpec((None, tb, ATTN_WIDTH), blk),
            _resident(bias.shape, lambda b, j: (0, 0, 0, 0)),
            _resident((D_MODEL, D_MODEL), lambda b, j: (0, 0)),
        ],
        out_specs=pl.BlockSpec((None, tb, D_MODEL), blk),
        out_shape=jax.ShapeDtypeStruct(x.shape, F32),
        scratch_shapes=[pltpu.VMEM((tb, ATTN_WIDTH), BF16)],
        compiler_params=pltpu.CompilerParams(
            dimension_semantics=("arbitrary", "arbitrary"), vmem_limit_bytes=VMEM_LIMIT),
        name="mixer",
    )(x, pool, q, k, v, ga, bias, wout_bf)


def _attention_bias(rpb):
    col = np.arange(GRID_W)
    blocks = np.array([_KEY_BLOCKS[m] for m in range(GRID_W // 8)])[col // 8]
    kc = COL_BLOCK * blocks[:, :, None] + np.arange(COL_BLOCK)[None, None, :]
    col_start = np.clip(col - WIN_COLS // 2, 0, GRID_W - WIN_COLS)[:, None, None]
    valid = ((kc >= col_start) & (kc < col_start + WIN_COLS)).astype(np.float32)
    valid = np.broadcast_to(valid[:, :, None, :], (GRID_W, 2, WIN_ROWS, COL_BLOCK))
    valid = valid.reshape(N_COL_BLOCKS, 1, COL_BLOCK, 2 * LANES)
    valid = np.broadcast_to(valid, (N_COL_BLOCKS, 2, COL_BLOCK, 2 * LANES)).reshape(2 * GRID_W,
                                                                                    2 * LANES)
    rpb_pad = jnp.pad(rpb, ((0, 0), (0, 0), (0, LANES - rpb.shape[-1])))
    return pl.pallas_call(
        _bias_kernel,
        out_shape=jax.ShapeDtypeStruct((N_PAIRS, WIN_ROWS, 2 * GRID_W, 2 * LANES), F32),
        name="bias_table",
    )(rpb_pad, jnp.asarray(valid))


def _bias_kernel(rpb_ref, valid_ref, out_ref):
    lane = lax.broadcasted_iota(jnp.int32, (8, LANES), 1)
    valid = valid_ref[...] > 0.0

    def one_table(idx, carry):
        p, delta = idx // WIN_ROWS, idx % WIN_ROWS
        acc = {}
        for hh in range(2):
            for i in range(WIN_ROWS):
                row = rpb_ref[2 * p + hh, pl.ds(i - delta + (WIN_ROWS - 1), 1), :]
                xb = jnp.broadcast_to(row, (8, LANES))
                in_row_i = (lane >= i * COL_BLOCK) & (lane < (i + 1) * COL_BLOCK)
                for m in range(GRID_W // 8):
                    for half, n in enumerate(_KEY_BLOCKS[m]):
                        shift = (COL_BLOCK * (i - n) + 8 * m - (WIN_COLS - 1)) % LANES
                        rolled = pltpu.roll(xb, shift, 1, stride=1, stride_axis=0)
                        key = (m, hh, half)
                        acc[key] = rolled if i == 0 else jnp.where(in_row_i, rolled, acc[key])
        tile = jnp.concatenate(
            [jnp.concatenate([acc[(2 * g + mm, hh, 0)], acc[(2 * g + mm, hh, 1)]], axis=1)
             for g in range(N_COL_BLOCKS) for hh in range(2) for mm in range(2)], axis=0)
        out_ref[p, delta] = jnp.where(valid, tile * LOG2E, MASKED)
        return carry

    lax.fori_loop(0, N_PAIRS * WIN_ROWS, one_table, 0)


def _encoder_layer(x, norm_g, w_in, w_pool, pool_scale, q_norm_g, k_norm_g, rpb, w_out):
    bsz, seq_len, _ = x.shape
    assert seq_len % BLOCK_TOKENS == 0 and seq_len // GRID_W >= KV_ROWS
    assert seq_len % PROJ_TOKENS == 0
    qg = jnp.tile(q_norm_g, N_HEADS)[None, :]
    kg = jnp.tile(k_norm_g, N_HEADS)[None, :]
    pool, q, k, v, ga = _proj_call(x.reshape(bsz * seq_len, D_MODEL), seq_len, norm_g[None, :],
                                   w_in.astype(BF16), qg, kg, w_pool.astype(BF16),
                                   pool_scale[None, :])
    shp = (bsz, seq_len, 512)
    return _mixer_call(x, pool.reshape(shp), q.reshape(shp), k.reshape(shp), v.reshape(shp),
                       ga.reshape(shp), _attention_bias(rpb), w_out.astype(BF16))


def kernel(x_prompt, x_sample, norm_g, w_in, w_pool, pool_scale, q_norm_g, k_norm_g, rpb, w_out):
    y_prompt, y_sample = x_prompt, x_sample
    for l in range(norm_g.shape[0]):
        params = (norm_g[l], w_in[l], w_pool[l], pool_scale[l], q_norm_g[l], k_norm_g[l],
                  rpb[l], w_out[l])
        y_prompt = _encoder_layer(y_prompt, *params)
        y_sample = _encoder_layer(y_sample, *params)
    return (y_prompt, y_sample)
```

```python
import functools
import math

import jax
import jax.numpy as jnp
import numpy as np
from jax import lax
from jax.experimental import pallas as pl
from jax.experimental.pallas import tpu as pltpu

D_MODEL = 1024
POOL_WIDTH = 512
ATTN_WIDTH = 512
POOL_WINDOWS = (2, 4, 8, 16)
POOL_GROUP_DIM = POOL_WIDTH // len(POOL_WINDOWS)
HEAD_DIM = 64
N_HEADS = ATTN_WIDTH // HEAD_DIM
GRID_W = 64
WIN_ROWS = 8
WIN_COLS = 16
IN_WIDTH = 2 * POOL_WIDTH + 4 * ATTN_WIDTH
EPS = 1e-6
LOG2E = math.log2(math.e)

LANES = 128
N_PAIRS = ATTN_WIDTH // LANES
HALO = 16
MASKED = -1e30
COL_BLOCK = 16
N_COL_BLOCKS = GRID_W // COL_BLOCK
_KEY_BLOCKS = {0: (0, 1), 1: (0, 1), 2: (0, 1), 3: (1, 2), 4: (1, 2), 5: (2, 3), 6: (2, 3),
               7: (2, 3)}

PROJ_TOKENS = 1024
BLOCK_ROWS = 16
BLOCK_TOKENS = BLOCK_ROWS * GRID_W
KV_ROWS = BLOCK_ROWS + WIN_ROWS
PROJ_CHUNK_ROWS = 4
ROW_GROUP = 8
KT_GROUPS = BLOCK_ROWS // ROW_GROUP + 2
N_BIAS_TABLES = 2 * WIN_ROWS - 1
VMEM_LIMIT = 56 * 1024 * 1024

F32 = jnp.float32
BF16 = jnp.bfloat16


def _resident(shape, index_map):
    return pl.BlockSpec(shape, index_map, pipeline_mode=pl.Buffered(1))


def _silu(g):
    return g * (1.0 / (1.0 + jnp.exp(-g)))


def _head_rms_norm(blk, gain, low_half):
    sq = blk * blk
    s_lo = jnp.sum(jnp.where(low_half, sq, 0.0), axis=-1, keepdims=True)
    s_hi = jnp.sum(jnp.where(low_half, 0.0, sq), axis=-1, keepdims=True)
    ms = jnp.where(low_half, s_lo, s_hi) * (1.0 / HEAD_DIM)
    return blk * lax.rsqrt(ms + EPS) * gain


def _proj_kernel(x_ref, xprev_ref, xnext_ref, ng_ref, win_ref, qg_ref, kg_ref, wpool_ref,
                 pscale_ref, pool_ref, q_ref, kt_ref, v_ref, ga_ref, uext_ref, s1_ref, s2_ref,
                 gate_ref, *, seq_len):
    assert POOL_WINDOWS == (2, 4, 8, 16)
    tm = x_ref.shape[0]
    tiles_per_seq = seq_len // tm
    pos = pl.program_id(0) % tiles_per_seq

    def normed(x):
        ms = jnp.mean(x * x, axis=-1, keepdims=True)
        return (x * lax.rsqrt(ms + EPS) * ng_ref[...]).astype(BF16)

    h = normed(x_ref[...])

    def seg(i):
        return jnp.dot(h, win_ref[:, i * 512:(i + 1) * 512], preferred_element_type=F32)

    h_ext = jnp.concatenate([normed(xprev_ref[...]), h, normed(xnext_ref[...])], axis=0)
    u_ext = jnp.dot(h_ext, win_ref[:, 0:POOL_WIDTH], preferred_element_type=F32)
    uext_ref[0:HALO, :] = jnp.where(pos > 0, u_ext[0:HALO], 0.0)
    uext_ref[HALO:HALO + tm, :] = u_ext[HALO:HALO + tm]
    n_ext = tm + 2 * HALO
    uext_ref[HALO + tm:n_ext, :] = jnp.where(pos < tiles_per_seq - 1, u_ext[HALO + tm:], 0.0)
    uext_ref[n_ext:, :] = jnp.zeros((8, POOL_WIDTH), F32)

    gd = POOL_GROUP_DIM
    s1_ref[0:n_ext, gd:] = uext_ref[0:n_ext, gd:] + uext_ref[1:n_ext + 1, gd:]
    s2_ref[0:n_ext - 8, 2 * gd:] = s1_ref[0:n_ext - 8, 2 * gd:] + s1_ref[2:n_ext - 6, 2 * gd:]
    s1_ref[0:n_ext - 16, 3 * gd:] = s2_ref[0:n_ext - 16, 3 * gd:] + s2_ref[4:n_ext - 12, 3 * gd:]
    level_refs = (uext_ref, s1_ref, s2_ref, s1_ref)
    pooled = []
    for g, w in enumerate(POOL_WINDOWS):
        a = w // 2
        b = w - a
        c = slice(g * gd, (g + 1) * gd)
        half_sums = level_refs[g]
        acc = half_sums[HALO - a:HALO - a + tm, c] + half_sums[HALO:HALO + tm, c]

        def edge_mean(first):
            t = pos * tm + first + lax.broadcasted_iota(jnp.int32, (HALO, 1), 0)
            count = (jnp.minimum(t + b, seq_len) - jnp.maximum(t - a, 0)).astype(F32)
            return acc[first:first + HALO] / count

        mean = jnp.concatenate([edge_mean(0), acc[HALO:tm - HALO] * (1.0 / w),
                                edge_mean(tm - HALO)], axis=0)
        pooled.append((mean - uext_ref[HALO:HALO + tm, c]).astype(BF16))

    low_half = lax.broadcasted_iota(jnp.int32, (tm, LANES), 1) < HEAD_DIM
    q = seg(2)
    for p in range(N_PAIRS):
        c = slice(p * LANES, (p + 1) * LANES)
        qn = _head_rms_norm(q[:, c], qg_ref[:, c], low_half) * (HEAD_DIM ** -0.5 * LOG2E)
        q_ref[:, c] = qn.astype(BF16)
    k = seg(3)
    for p in range(N_PAIRS):
        c = slice(p * LANES, (p + 1) * LANES)
        kn = _head_rms_norm(k[:, c], kg_ref[:, c], low_half)
        for rg in range(tm // (ROW_GROUP * GRID_W)):
            for n in range(N_COL_BLOCKS):
                keys = jnp.concatenate(
                    [kn[(ROW_GROUP * rg + i) * GRID_W + n * COL_BLOCK:
                        (ROW_GROUP * rg + i) * GRID_W + (n + 1) * COL_BLOCK]
                     for i in range(ROW_GROUP)], axis=0)
                kt_ref[p, n, rg] = keys.T.astype(BF16)

    gate_ref[...] = _silu(seg(1))
    ga_ref[...] = _silu(seg(5)).astype(BF16)
    v_ref[...] = seg(4).astype(BF16)

    for g in range(len(POOL_WINDOWS)):
        c = slice(g * POOL_GROUP_DIM, (g + 1) * POOL_GROUP_DIM)
        mixed = jnp.dot(pooled[g], wpool_ref[g], preferred_element_type=F32)
        pool_ref[:, c] = (mixed * pscale_ref[:, c] * gate_ref[:, c]).astype(BF16)


def _proj_call(x2d, seq_len, norm_g, w_in_bf, qg, kg, wpool_bf, pscale):
    n = x2d.shape[0]
    tm = PROJ_TOKENS
    tiles_per_seq = seq_len // tm
    groups_per_tile = tm // (ROW_GROUP * GRID_W)
    halo_blocks = tm // HALO
    n_halo = n // HALO
    out = jax.ShapeDtypeStruct((n, 512), BF16)
    kt_out = jax.ShapeDtypeStruct(
        (n // seq_len, N_PAIRS, N_COL_BLOCKS, seq_len // (ROW_GROUP * GRID_W), LANES, LANES), BF16)
    kt_spec = pl.BlockSpec((None, N_PAIRS, N_COL_BLOCKS, groups_per_tile, LANES, LANES),
                           lambda i: (i // tiles_per_seq, 0, 0, i % tiles_per_seq, 0, 0))
    row = lambda i: (i, 0)
    fixed2 = lambda i: (0, 0)
    tok_spec = pl.BlockSpec((tm, 512), row)
    return pl.pallas_call(
        functools.partial(_proj_kernel, seq_len=seq_len),
        grid=(n // tm,),
        in_specs=[
            pl.BlockSpec((tm, D_MODEL), row),
            pl.BlockSpec((HALO, D_MODEL), lambda i: (jnp.maximum(i * halo_blocks - 1, 0), 0)),
            pl.BlockSpec((HALO, D_MODEL),
                         lambda i: (jnp.minimum((i + 1) * halo_blocks, n_halo - 1), 0)),
            _resident((1, D_MODEL), fixed2),
            _resident((D_MODEL, IN_WIDTH), fixed2),
            _resident((1, ATTN_WIDTH), fixed2),
            _resident((1, ATTN_WIDTH), fixed2),
            _resident(wpool_bf.shape, lambda i: (0, 0, 0)),
            _resident((1, POOL_WIDTH), fixed2),
        ],
        out_specs=[tok_spec, tok_spec, kt_spec, tok_spec, tok_spec],
        out_shape=[out, out, kt_out, out, out],
        scratch_shapes=[pltpu.VMEM((tm + 2 * HALO + 8, POOL_WIDTH), F32)] * 3
        + [pltpu.VMEM((tm, POOL_WIDTH), F32)],
        compiler_params=pltpu.CompilerParams(
            dimension_semantics=("arbitrary",), vmem_limit_bytes=VMEM_LIMIT),
        name="proj",
    )(x2d, x2d, x2d, norm_g, w_in_bf, qg, kg, wpool_bf, pscale)


def _mixer_kernel(x_ref, pool_ref, q_ref, kt_ref, v_ref, ga_ref, bias_ref, wout_ref, y_ref,
                  attn_ref, *, seq_len):
    j = pl.program_id(1)
    rows = seq_len // GRID_W
    low_half = lax.broadcasted_iota(jnp.int32, (GRID_W, LANES), 1) < HEAD_DIM
    low16 = lax.broadcasted_iota(jnp.int32, (COL_BLOCK, LANES), 1) < HEAD_DIM

    pair_cols = [slice(p * LANES, (p + 1) * LANES) for p in range(N_PAIRS)]
    zero_blk = jnp.zeros((8, LANES), F32)
    key_lane = lax.broadcasted_iota(jnp.int32, (LANES, LANES), 1).astype(F32).astype(BF16)

    kv_first_row = _kv_first_row(j, rows)
    kt_first_group = _kt_first_group(j, rows)

    def window(i):
        r = j * BLOCK_ROWS + i
        rs = jnp.clip(r - WIN_ROWS // 2, 0, rows - WIN_ROWS)
        return r - rs, rs

    def key_tiles(rs, p):
        off = rs % ROW_GROUP
        g0 = rs // ROW_GROUP - kt_first_group
        g1 = jnp.minimum(g0 + 1, KT_GROUPS - 1)
        from_first = key_lane >= (off * COL_BLOCK).astype(F32).astype(BF16)
        return [jnp.where(from_first, kt_ref[0, p, n, g0], kt_ref[0, p, n, g1])
                for n in range(N_COL_BLOCKS)]

    def value_rows(rs, c):
        off = rs % ROW_GROUP
        pieces = []
        for n in range(N_COL_BLOCKS):
            for i8 in range(ROW_GROUP):
                row = rs - kv_first_row + ((i8 - off) & (ROW_GROUP - 1))
                start = pl.multiple_of(row * GRID_W + n * COL_BLOCK, COL_BLOCK)
                pieces.append(v_ref[0, pl.ds(start, COL_BLOCK), c])
        return jnp.concatenate(pieces, axis=0)

    def scores(i):
        _, rs = window(i)
        out = []
        for p, c in enumerate(pair_cols):
            q2 = q_ref[i * GRID_W:(i + 1) * GRID_W, c]
            zero = jnp.zeros_like(q2)
            qa = jnp.where(low_half, q2, zero)
            qb = jnp.where(low_half, zero, q2)
            qq = jnp.concatenate(
                [blk[g * COL_BLOCK:(g + 1) * COL_BLOCK] for g in range(N_COL_BLOCKS)
                 for blk in (qa, qb)], axis=0)
            kt = key_tiles(rs, p)
            d0 = jnp.dot(qq[:96], jnp.concatenate(kt[:2], axis=1), preferred_element_type=F32)
            d1 = jnp.dot(qq[32:], jnp.concatenate(kt[2:], axis=1), preferred_element_type=F32)
            pieces = []
            for m8 in range(2 * N_COL_BLOCKS * 2):
                g, hh, mm = m8 // 4, (m8 // 2) % 2, m8 % 2
                row = []
                for n in _KEY_BLOCKS[2 * g + mm]:
                    src, r0, l0 = (d0, m8 * 8, n * LANES) if n < 2 else (d1, m8 * 8 - 32,
                                                                         (n - 2) * LANES)
                    row.append(src[r0:r0 + 8, l0:l0 + LANES])
                pieces.append(jnp.concatenate(row, axis=1))
            out.append(jnp.concatenate(pieces, axis=0))
        return out

    def softmax(i, s_list):
        delta, rs = window(i)
        off = rs % ROW_GROUP
        table = jnp.where(delta != WIN_ROWS // 2, delta,
                          jnp.where(off == 0, WIN_ROWS // 2, WIN_ROWS - 1 + off))
        out = []
        for p, s in enumerate(s_list):
            s = s + bias_ref[p, table]
            e = jnp.exp2(s - jnp.max(s, axis=-1, keepdims=True))
            inv_l = 1.0 / jnp.sum(e, axis=-1, keepdims=True)
            halves = []
            for t in range(2):
                rows8 = []
                for m8 in range(4 * t, 4 * t + 12):
                    blocks = _KEY_BLOCKS[2 * (m8 // 4) + m8 % 2]
                    rows8.append(jnp.concatenate(
                        [e[m8 * 8:m8 * 8 + 8, blocks.index(n) * LANES:(blocks.index(n) + 1) * LANES]
                         if n in blocks else zero_blk for n in (2 * t, 2 * t + 1)], axis=1))
                halves.append(jnp.concatenate(rows8, axis=0).astype(BF16))
            out.append((halves, inv_l))
        return out

    def weighted_values(i, e_list):
        _, rs = window(i)
        tok = slice(i * GRID_W, (i + 1) * GRID_W)
        for p, ((p0, p1), inv_l) in enumerate(e_list):
            c = pair_cols[p]
            vv = value_rows(rs, c)
            o0 = jnp.dot(p0, vv[:256], preferred_element_type=F32)
            o1 = jnp.dot(p1, vv[256:], preferred_element_type=F32)
            o = jnp.concatenate([o0[:32], o0[32:] + o1[:64], o1[64:]], axis=0) * inv_l
            o2 = jnp.concatenate(
                [jnp.where(low16, o[g * 32:g * 32 + 16], o[g * 32 + 16:g * 32 + 32])
                 for g in range(N_COL_BLOCKS)], axis=0)
            attn_ref[tok, c] = (o2 * ga_ref[tok, c].astype(F32)).astype(BF16)

    chunk_tokens = PROJ_CHUNK_ROWS * GRID_W
    n_col_chunks = D_MODEL // 512

    def proj_chunk(idx, src_ref, w_rows, first):
        tok = slice((idx // n_col_chunks) * chunk_tokens, (idx // n_col_chunks + 1) * chunk_tokens)
        cols = slice((idx % n_col_chunks) * 512, (idx % n_col_chunks + 1) * 512)
        upd = jnp.dot(src_ref[tok, :], wout_ref[w_rows, cols], preferred_element_type=F32)
        if first:
            y_ref[tok, cols] = x_ref[tok, cols] + upd
        else:
            y_ref[tok, cols] += upd

    n_chunks = (BLOCK_ROWS // PROJ_CHUNK_ROWS) * n_col_chunks
    attn_stage = [(idx // n_col_chunks + 1) * PROJ_CHUNK_ROWS + 2 + idx % n_col_chunks
                  for idx in range(n_chunks)]
    s_rows, e_rows = {}, {}
    for t in range(BLOCK_ROWS + 2):
        if t < BLOCK_ROWS:
            s_rows[t] = scores(t)
        if 1 <= t and (t - 1) % 2 == 0 and (t - 1) // 2 < n_chunks:
            proj_chunk((t - 1) // 2, pool_ref, slice(0, POOL_WIDTH), True)
        if 1 <= t <= BLOCK_ROWS:
            e_rows[t - 1] = softmax(t - 1, s_rows.pop(t - 1))
        if t >= 2:
            weighted_values(t - 2, e_rows.pop(t - 2))
        for idx in range(n_chunks):
            if attn_stage[idx] == t:
                proj_chunk(idx, attn_ref, slice(POOL_WIDTH, D_MODEL), False)
    for idx in range(n_chunks):
        if attn_stage[idx] > BLOCK_ROWS + 1:
            proj_chunk(idx, attn_ref, slice(POOL_WIDTH, D_MODEL), False)


def _kv_first_row(j, rows):
    return jnp.clip(j * BLOCK_ROWS - WIN_ROWS // 2, 0, rows - KV_ROWS)


def _kt_first_group(j, rows):
    return jnp.clip(j * (BLOCK_ROWS // ROW_GROUP) - 1, 0, rows // ROW_GROUP - KT_GROUPS)


def _mixer_call(x, pool, q, kt, v, ga, bias, wout_bf):
    bsz, seq_len, _ = x.shape
    tb = BLOCK_TOKENS
    rows = seq_len // GRID_W
    blk = lambda b, j: (b, j, 0)
    kt_spec = pl.BlockSpec(
        (pl.Element(1), pl.Element(N_PAIRS), pl.Element(N_COL_BLOCKS), pl.Element(KT_GROUPS),
         pl.Element(LANES), pl.Element(LANES)),
        lambda b, j: (b, 0, 0, _kt_first_group(j, rows), 0, 0))
    kv_spec = pl.BlockSpec(
        (pl.Element(1), pl.Element(KV_ROWS * GRID_W), pl.Element(ATTN_WIDTH)),
        lambda b, j: (b, _kv_first_row(j, rows) * GRID_W, 0))
    return pl.pallas_call(
        functools.partial(_mixer_kernel, seq_len=seq_len),
        grid=(bsz, seq_len // tb),
        in_specs=[
            pl.BlockSpec((None, tb, D_MODEL), blk),
            pl.BlockSpec((None, tb, POOL_WIDTH), blk),
            pl.BlockSpec((None, tb, ATTN_WIDTH), blk),
            kt_spec,
            kv_spec,
            pl.BlockSpec((None, tb, ATTN_WIDTH), blk),
            _resident(bias.shape, lambda b, j: (0, 0, 0, 0)),
            _resident((D_MODEL, D_MODEL), lambda b, j: (0, 0)),
        ],
        out_specs=pl.BlockSpec((None, tb, D_MODEL), blk),
        out_shape=jax.ShapeDtypeStruct(x.shape, F32),
        scratch_shapes=[pltpu.VMEM((tb, ATTN_WIDTH), BF16)],
        compiler_params=pltpu.CompilerParams(
            dimension_semantics=("arbitrary", "arbitrary"), vmem_limit_bytes=VMEM_LIMIT),
        name="mixer",
    )(x, pool, q, kt, v, ga, bias, wout_bf)


def _attention_bias(rpb):
    col = np.arange(GRID_W)
    blocks = np.array([_KEY_BLOCKS[m] for m in range(GRID_W // 8)])[col // 8]
    kc = COL_BLOCK * blocks[:, :, None] + np.arange(COL_BLOCK)[None, None, :]
    col_start = np.clip(col - WIN_COLS // 2, 0, GRID_W - WIN_COLS)[:, None, None]
    valid = ((kc >= col_start) & (kc < col_start + WIN_COLS)).astype(np.float32)
    valid = np.broadcast_to(valid[:, :, None, :], (GRID_W, 2, WIN_ROWS, COL_BLOCK))
    valid = valid.reshape(N_COL_BLOCKS, 1, COL_BLOCK, 2 * LANES)
    valid = np.broadcast_to(valid, (N_COL_BLOCKS, 2, COL_BLOCK, 2 * LANES)).reshape(2 * GRID_W,
                                                                                    2 * LANES)
    rpb_pad = jnp.pad(rpb, ((0, 0), (0, 0), (0, LANES - rpb.shape[-1])))
    return pl.pallas_call(
        _bias_kernel,
        out_shape=jax.ShapeDtypeStruct((N_PAIRS, N_BIAS_TABLES, 2 * GRID_W, 2 * LANES), F32),
        name="bias_table",
    )(rpb_pad, jnp.asarray(valid))


def _bias_kernel(rpb_ref, valid_ref, out_ref):
    lane = lax.broadcasted_iota(jnp.int32, (8, LANES), 1)
    valid = valid_ref[...] > 0.0

    def table(p, delta, off):
        acc = {}
        for hh in range(2):
            for i in range(WIN_ROWS):
                grp = (i + off) % ROW_GROUP
                row = rpb_ref[2 * p + hh, pl.ds(i - delta + (WIN_ROWS - 1), 1), :]
                xb = jnp.broadcast_to(row, (8, LANES))
                in_grp = (lane >= grp * COL_BLOCK) & (lane < (grp + 1) * COL_BLOCK)
                for m in range(GRID_W // 8):
                    for half, n in enumerate(_KEY_BLOCKS[m]):
                        shift = (COL_BLOCK * (grp - n) + 8 * m - (WIN_COLS - 1)) % LANES
                        rolled = pltpu.roll(xb, shift, 1, stride=1, stride_axis=0)
                        key = (m, hh, half)
                        acc[key] = rolled if i == 0 else jnp.where(in_grp, rolled, acc[key])
        tile = jnp.concatenate(
            [jnp.concatenate([acc[(2 * g + mm, hh, 0)], acc[(2 * g + mm, hh, 1)]], axis=1)
             for g in range(N_COL_BLOCKS) for hh in range(2) for mm in range(2)], axis=0)
        return jnp.where(valid, tile * LOG2E, MASKED)

    def aligned(idx, carry):
        p, delta = idx // WIN_ROWS, idx % WIN_ROWS
        out_ref[p, delta] = table(p, delta, 0)
        return carry

    lax.fori_loop(0, N_PAIRS * WIN_ROWS, aligned, 0)
    for off in range(1, ROW_GROUP):
        def shifted(p, carry, off=off):
            out_ref[p, WIN_ROWS - 1 + off] = table(p, WIN_ROWS // 2, off)
            return carry

        lax.fori_loop(0, N_PAIRS, shifted, 0)


def _encoder_layer(x, norm_g, w_in, w_pool, pool_scale, q_norm_g, k_norm_g, rpb, w_out):
    bsz, seq_len, _ = x.shape
    assert seq_len % BLOCK_TOKENS == 0 and seq_len // GRID_W >= KV_ROWS
    assert seq_len % PROJ_TOKENS == 0 and PROJ_TOKENS % (ROW_GROUP * GRID_W) == 0
    assert seq_len // (ROW_GROUP * GRID_W) >= KT_GROUPS and ROW_GROUP == WIN_ROWS
    qg = jnp.tile(q_norm_g, N_HEADS)[None, :]
    kg = jnp.tile(k_norm_g, N_HEADS)[None, :]
    pool, q, kt, v, ga = _proj_call(x.reshape(bsz * seq_len, D_MODEL), seq_len, norm_g[None, :],
                                    w_in.astype(BF16), qg, kg, w_pool.astype(BF16),
                                    pool_scale[None, :])
    shp = (bsz, seq_len, 512)
    return _mixer_call(x, pool.reshape(shp), q.reshape(shp), kt, v.reshape(shp),
                       ga.reshape(shp), _attention_bias(rpb), w_out.astype(BF16))


def kernel(x_prompt, x_sample, norm_g, w_in, w_pool, pool_scale, q_norm_g, k_norm_g, rpb, w_out):
    y_prompt, y_sample = x_prompt, x_sample
    for l in range(norm_g.shape[0]):
        params = (norm_g[l], w_in[l], w_pool[l], pool_scale[l], q_norm_g[l], k_norm_g[l],
                  rpb[l], w_out[l])
        y_prompt = _encoder_layer(y_prompt, *params)
        y_sample = _encoder_layer(y_sample, *params)
    return (y_prompt, y_sample)
```

```python
import functools
import math

import jax
import jax.numpy as jnp
import numpy as np
from jax import lax
from jax.experimental import pallas as pl
from jax.experimental.pallas import tpu as pltpu

D_MODEL = 1024
POOL_WIDTH = 512
ATTN_WIDTH = 512
POOL_WINDOWS = (2, 4, 8, 16)
POOL_GROUP_DIM = POOL_WIDTH // len(POOL_WINDOWS)
HEAD_DIM = 64
N_HEADS = ATTN_WIDTH // HEAD_DIM
GRID_W = 64
WIN_ROWS = 8
WIN_COLS = 16
IN_WIDTH = 2 * POOL_WIDTH + 4 * ATTN_WIDTH
EPS = 1e-6
LOG2E = math.log2(math.e)

LANES = 128
N_PAIRS = ATTN_WIDTH // LANES
HALO = 16
MASKED = -1e30
COL_BLOCK = 16
N_COL_BLOCKS = GRID_W // COL_BLOCK
_KEY_BLOCKS = {0: (0, 1), 1: (0, 1), 2: (0, 1), 3: (1, 2), 4: (1, 2), 5: (2, 3), 6: (2, 3),
               7: (2, 3)}

PROJ_TOKENS = 1024
BLOCK_ROWS = 16
BLOCK_TOKENS = BLOCK_ROWS * GRID_W
KV_ROWS = BLOCK_ROWS + WIN_ROWS
PROJ_CHUNK_ROWS = 4
PV_LAG = 2
ROW_GROUP = 8
KT_GROUPS = BLOCK_ROWS // ROW_GROUP + 2
N_BIAS_TABLES = 2 * WIN_ROWS - 1
VMEM_LIMIT = 56 * 1024 * 1024

F32 = jnp.float32
BF16 = jnp.bfloat16


def _resident(shape, index_map):
    return pl.BlockSpec(shape, index_map, pipeline_mode=pl.Buffered(1))


def _silu(g):
    return g * (1.0 / (1.0 + jnp.exp(-g)))


def _head_rms_norm(blk, gain, low_half):
    sq = blk * blk
    s_lo = jnp.sum(jnp.where(low_half, sq, 0.0), axis=-1, keepdims=True)
    s_hi = jnp.sum(jnp.where(low_half, 0.0, sq), axis=-1, keepdims=True)
    ms = jnp.where(low_half, s_lo, s_hi) * (1.0 / HEAD_DIM)
    return blk * lax.rsqrt(ms + EPS) * gain


def _proj_kernel(x_ref, xprev_ref, xnext_ref, ng_ref, win_ref, qg_ref, kg_ref, wpool_ref,
                 pscale_ref, pool_ref, q_ref, kt_ref, v_ref, ga_ref, uext_ref, s1_ref, s2_ref,
                 gate_ref, *, seq_len):
    assert POOL_WINDOWS == (2, 4, 8, 16)
    tm = x_ref.shape[0]
    tiles_per_seq = seq_len // tm
    pos = pl.program_id(0) % tiles_per_seq

    def normed(x):
        ms = jnp.mean(x * x, axis=-1, keepdims=True)
        return (x * lax.rsqrt(ms + EPS) * ng_ref[...]).astype(BF16)

    quarter = tm // 4
    h_parts = [normed(x_ref[c * quarter:(c + 1) * quarter, :]) for c in range(4)]
    h = jnp.concatenate(h_parts, axis=0)

    def seg(i):
        return jnp.dot(h, win_ref[:, i * 512:(i + 1) * 512], preferred_element_type=F32)

    n_ext = tm + 2 * HALO
    w_u = win_ref[:, 0:POOL_WIDTH]
    first = jnp.dot(jnp.concatenate([normed(xprev_ref[...]), h_parts[0]], axis=0), w_u,
                    preferred_element_type=F32)
    uext_ref[0:HALO, :] = jnp.where(pos > 0, first[0:HALO], 0.0)
    uext_ref[HALO:HALO + quarter, :] = first[HALO:]
    for c in (1, 2):
        uext_ref[HALO + c * quarter:HALO + (c + 1) * quarter, :] = jnp.dot(
            h_parts[c], w_u, preferred_element_type=F32)
    last = jnp.dot(jnp.concatenate([h_parts[3], normed(xnext_ref[...])], axis=0), w_u,
                   preferred_element_type=F32)
    uext_ref[HALO + 3 * quarter:HALO + tm, :] = last[:quarter]
    uext_ref[HALO + tm:n_ext, :] = jnp.where(pos < tiles_per_seq - 1, last[quarter:], 0.0)
    uext_ref[n_ext:, :] = jnp.zeros((8, POOL_WIDTH), F32)

    gd = POOL_GROUP_DIM
    s1_ref[0:n_ext, gd:] = uext_ref[0:n_ext, gd:] + uext_ref[1:n_ext + 1, gd:]
    s2_ref[0:n_ext - 8, 2 * gd:] = s1_ref[0:n_ext - 8, 2 * gd:] + s1_ref[2:n_ext - 6, 2 * gd:]
    s1_ref[0:n_ext - 16, 3 * gd:] = s2_ref[0:n_ext - 16, 3 * gd:] + s2_ref[4:n_ext - 12, 3 * gd:]
    level_refs = (uext_ref, s1_ref, s2_ref, s1_ref)
    pooled = []
    for g, w in enumerate(POOL_WINDOWS):
        a = w // 2
        b = w - a
        c = slice(g * gd, (g + 1) * gd)
        half_sums = level_refs[g]
        acc = half_sums[HALO - a:HALO - a + tm, c] + half_sums[HALO:HALO + tm, c]

        def edge_mean(first):
            t = pos * tm + first + lax.broadcasted_iota(jnp.int32, (HALO, 1), 0)
            count = (jnp.minimum(t + b, seq_len) - jnp.maximum(t - a, 0)).astype(F32)
            return acc[first:first + HALO] / count

        mean = jnp.concatenate([edge_mean(0), acc[HALO:tm - HALO] * (1.0 / w),
                                edge_mean(tm - HALO)], axis=0)
        pooled.append((mean - uext_ref[HALO:HALO + tm, c]).astype(BF16))

    low_half = lax.broadcasted_iota(jnp.int32, (tm, LANES), 1) < HEAD_DIM
    q = seg(2)
    for p in range(N_PAIRS):
        c = slice(p * LANES, (p + 1) * LANES)
        qn = _head_rms_norm(q[:, c], qg_ref[:, c], low_half) * (HEAD_DIM ** -0.5 * LOG2E)
        q_ref[:, c] = qn.astype(BF16)
    k = seg(3)
    for p in range(N_PAIRS):
        c = slice(p * LANES, (p + 1) * LANES)
        kn = _head_rms_norm(k[:, c], kg_ref[:, c], low_half)
        for rg in range(tm // (ROW_GROUP * GRID_W)):
            for n in range(N_COL_BLOCKS):
                keys = jnp.concatenate(
                    [kn[(ROW_GROUP * rg + i) * GRID_W + n * COL_BLOCK:
                        (ROW_GROUP * rg + i) * GRID_W + (n + 1) * COL_BLOCK]
                     for i in range(ROW_GROUP)], axis=0)
                kt_ref[p, n, rg] = keys.T.astype(BF16)

    gate_ref[...] = _silu(seg(1))
    ga_ref[...] = _silu(seg(5)).astype(BF16)
    v_ref[...] = seg(4).astype(BF16)

    for g in range(len(POOL_WINDOWS)):
        c = slice(g * POOL_GROUP_DIM, (g + 1) * POOL_GROUP_DIM)
        mixed = jnp.dot(pooled[g], wpool_ref[g], preferred_element_type=F32)
        pool_ref[:, c] = (mixed * pscale_ref[:, c] * gate_ref[:, c]).astype(BF16)


def _proj_call(x2d, seq_len, norm_g, w_in_bf, qg, kg, wpool_bf, pscale):
    n = x2d.shape[0]
    tm = PROJ_TOKENS
    tiles_per_seq = seq_len // tm
    groups_per_tile = tm // (ROW_GROUP * GRID_W)
    halo_blocks = tm // HALO
    n_halo = n // HALO
    out = jax.ShapeDtypeStruct((n, 512), BF16)
    kt_out = jax.ShapeDtypeStruct(
        (n // seq_len, N_PAIRS, N_COL_BLOCKS, seq_len // (ROW_GROUP * GRID_W), LANES, LANES), BF16)
    kt_spec = pl.BlockSpec((None, N_PAIRS, N_COL_BLOCKS, groups_per_tile, LANES, LANES),
                           lambda i: (i // tiles_per_seq, 0, 0, i % tiles_per_seq, 0, 0))
    row = lambda i: (i, 0)
    fixed2 = lambda i: (0, 0)
    tok_spec = pl.BlockSpec((tm, 512), row)
    return pl.pallas_call(
        functools.partial(_proj_kernel, seq_len=seq_len),
        grid=(n // tm,),
        in_specs=[
            pl.BlockSpec((tm, D_MODEL), row),
            pl.BlockSpec((HALO, D_MODEL), lambda i: (jnp.maximum(i * halo_blocks - 1, 0), 0)),
            pl.BlockSpec((HALO, D_MODEL),
                         lambda i: (jnp.minimum((i + 1) * halo_blocks, n_halo - 1), 0)),
            _resident((1, D_MODEL), fixed2),
            _resident((D_MODEL, IN_WIDTH), fixed2),
            _resident((1, ATTN_WIDTH), fixed2),
            _resident((1, ATTN_WIDTH), fixed2),
            _resident(wpool_bf.shape, lambda i: (0, 0, 0)),
            _resident((1, POOL_WIDTH), fixed2),
        ],
        out_specs=[tok_spec, tok_spec, kt_spec, tok_spec, tok_spec],
        out_shape=[out, out, kt_out, out, out],
        scratch_shapes=[pltpu.VMEM((tm + 2 * HALO + 8, POOL_WIDTH), F32)] * 3
        + [pltpu.VMEM((tm, POOL_WIDTH), F32)],
        compiler_params=pltpu.CompilerParams(
            dimension_semantics=("arbitrary",), vmem_limit_bytes=VMEM_LIMIT),
        name="proj",
    )(x2d, x2d, x2d, norm_g, w_in_bf, qg, kg, wpool_bf, pscale)


def _mixer_kernel(x_ref, pool_ref, q_ref, kt_ref, v_ref, ga_ref, bias_ref, wout_ref, y_ref,
                  attn_ref, *, seq_len):
    j = pl.program_id(1)
    rows = seq_len // GRID_W
    low_half = lax.broadcasted_iota(jnp.int32, (GRID_W, LANES), 1) < HEAD_DIM
    low16 = lax.broadcasted_iota(jnp.int32, (COL_BLOCK, LANES), 1) < HEAD_DIM

    pair_cols = [slice(p * LANES, (p + 1) * LANES) for p in range(N_PAIRS)]
    zero_blk = jnp.zeros((8, LANES), F32)
    key_lane = lax.broadcasted_iota(jnp.int32, (LANES, LANES), 1).astype(F32).astype(BF16)

    kv_first_row = _kv_first_row(j, rows)
    kt_first_group = _kt_first_group(j, rows)

    def window(i):
        r = j * BLOCK_ROWS + i
        rs = jnp.clip(r - WIN_ROWS // 2, 0, rows - WIN_ROWS)
        return r - rs, rs

    def key_tiles(rs, p):
        off = rs % ROW_GROUP
        g0 = rs // ROW_GROUP - kt_first_group
        g1 = jnp.minimum(g0 + 1, KT_GROUPS - 1)
        from_first = key_lane >= (off * COL_BLOCK).astype(F32).astype(BF16)
        return [jnp.where(from_first, kt_ref[0, p, n, g0], kt_ref[0, p, n, g1])
                for n in range(N_COL_BLOCKS)]

    def value_rows(rs, c):
        off = rs % ROW_GROUP
        pieces = []
        for n in range(N_COL_BLOCKS):
            for i8 in range(ROW_GROUP):
                row = rs - kv_first_row + ((i8 - off) & (ROW_GROUP - 1))
                start = pl.multiple_of(row * GRID_W + n * COL_BLOCK, COL_BLOCK)
                pieces.append(v_ref[0, pl.ds(start, COL_BLOCK), c])
        return jnp.concatenate(pieces, axis=0)

    def scores(i):
        _, rs = window(i)
        out = []
        for p, c in enumerate(pair_cols):
            q2 = q_ref[i * GRID_W:(i + 1) * GRID_W, c]
            zero = jnp.zeros_like(q2)
            qa = jnp.where(low_half, q2, zero)
            qb = jnp.where(low_half, zero, q2)
            qq = jnp.concatenate(
                [blk[g * COL_BLOCK:(g + 1) * COL_BLOCK] for g in range(N_COL_BLOCKS)
                 for blk in (qa, qb)], axis=0)
            kt = key_tiles(rs, p)
            d0 = jnp.dot(qq[:96], jnp.concatenate(kt[:2], axis=1), preferred_element_type=F32)
            d1 = jnp.dot(qq[32:], jnp.concatenate(kt[2:], axis=1), preferred_element_type=F32)
            pieces = []
            for m8 in range(2 * N_COL_BLOCKS * 2):
                g, hh, mm = m8 // 4, (m8 // 2) % 2, m8 % 2
                row = []
                for n in _KEY_BLOCKS[2 * g + mm]:
                    src, r0, l0 = (d0, m8 * 8, n * LANES) if n < 2 else (d1, m8 * 8 - 32,
                                                                         (n - 2) * LANES)
                    row.append(src[r0:r0 + 8, l0:l0 + LANES])
                pieces.append(jnp.concatenate(row, axis=1))
            out.append(jnp.concatenate(pieces, axis=0))
        return out

    def softmax(i, s_list):
        delta, rs = window(i)
        off = rs % ROW_GROUP
        table = jnp.where(delta != WIN_ROWS // 2, delta,
                          jnp.where(off == 0, WIN_ROWS // 2, WIN_ROWS - 1 + off))
        out = []
        for p, s in enumerate(s_list):
            s = s + bias_ref[p, table]
            e = jnp.exp2(s - jnp.max(s, axis=-1, keepdims=True))
            inv_l = 1.0 / jnp.sum(e, axis=-1, keepdims=True)
            halves = []
            for t in range(2):
                rows8 = []
                for m8 in range(4 * t, 4 * t + 12):
                    blocks = _KEY_BLOCKS[2 * (m8 // 4) + m8 % 2]
                    rows8.append(jnp.concatenate(
                        [e[m8 * 8:m8 * 8 + 8, blocks.index(n) * LANES:(blocks.index(n) + 1) * LANES]
                         if n in blocks else zero_blk for n in (2 * t, 2 * t + 1)], axis=1))
                halves.append(jnp.concatenate(rows8, axis=0).astype(BF16))
            out.append((halves, inv_l))
        return out

    def weighted_values(i, e_list):
        _, rs = window(i)
        tok = slice(i * GRID_W, (i + 1) * GRID_W)
        for p, ((p0, p1), inv_l) in enumerate(e_list):
            c = pair_cols[p]
            vv = value_rows(rs, c)
            o0 = jnp.dot(p0, vv[:256], preferred_element_type=F32)
            o1 = jnp.dot(p1, vv[256:], preferred_element_type=F32)
            o = jnp.concatenate([o0[:32], o0[32:] + o1[:64], o1[64:]], axis=0) * inv_l
            o2 = jnp.concatenate(
                [jnp.where(low16, o[g * 32:g * 32 + 16], o[g * 32 + 16:g * 32 + 32])
                 for g in range(N_COL_BLOCKS)], axis=0)
            attn_ref[tok, c] = (o2 * ga_ref[tok, c].astype(F32)).astype(BF16)

    chunk_tokens = PROJ_CHUNK_ROWS * GRID_W
    n_col_chunks = D_MODEL // 512

    def proj_chunk(idx, src_ref, w_rows, first):
        tok = slice((idx // n_col_chunks) * chunk_tokens, (idx // n_col_chunks + 1) * chunk_tokens)
        cols = slice((idx % n_col_chunks) * 512, (idx % n_col_chunks + 1) * 512)
        upd = jnp.dot(src_ref[tok, :], wout_ref[w_rows, cols], preferred_element_type=F32)
        if first:
            y_ref[tok, cols] = x_ref[tok, cols] + upd
        else:
            y_ref[tok, cols] += upd

    n_chunks = (BLOCK_ROWS // PROJ_CHUNK_ROWS) * n_col_chunks
    attn_stage = [(idx // n_col_chunks + 1) * PROJ_CHUNK_ROWS + PV_LAG + idx % n_col_chunks
                  for idx in range(n_chunks)]
    last_stage = BLOCK_ROWS + PV_LAG - 1
    s_rows, e_rows = {}, {}
    for t in range(last_stage + 1):
        if t < BLOCK_ROWS:
            s_rows[t] = scores(t)
        for idx in range(n_chunks):
            if 1 + idx * BLOCK_ROWS // n_chunks == t:
                proj_chunk(idx, pool_ref, slice(0, POOL_WIDTH), True)
        if 1 <= t <= BLOCK_ROWS:
            e_rows[t - 1] = softmax(t - 1, s_rows.pop(t - 1))
        if t >= PV_LAG:
            weighted_values(t - PV_LAG, e_rows.pop(t - PV_LAG))
        for idx in range(n_chunks):
            if attn_stage[idx] == t:
                proj_chunk(idx, attn_ref, slice(POOL_WIDTH, D_MODEL), False)
    for idx in range(n_chunks):
        if attn_stage[idx] > last_stage:
            proj_chunk(idx, attn_ref, slice(POOL_WIDTH, D_MODEL), False)


def _kv_first_row(j, rows):
    return jnp.clip(j * BLOCK_ROWS - WIN_ROWS // 2, 0, rows - KV_ROWS)


def _kt_first_group(j, rows):
    return jnp.clip(j * (BLOCK_ROWS // ROW_GROUP) - 1, 0, rows // ROW_GROUP - KT_GROUPS)


def _mixer_call(x, pool, q, kt, v, ga, bias, wout_bf):
    bsz, seq_len, _ = x.shape
    tb = BLOCK_TOKENS
    rows = seq_len // GRID_W
    blk = lambda b, j: (b, j, 0)
    kt_spec = pl.BlockSpec(
        (pl.Element(1), pl.Element(N_PAIRS), pl.Element(N_COL_BLOCKS), pl.Element(KT_GROUPS),
         pl.Element(LANES), pl.Element(LANES)),
        lambda b, j: (b, 0, 0, _kt_first_group(j, rows), 0, 0))
    kv_spec = pl.BlockSpec(
        (pl.Element(1), pl.Element(KV_ROWS * GRID_W), pl.Element(ATTN_WIDTH)),
        lambda b, j: (b, _kv_first_row(j, rows) * GRID_W, 0))
    return pl.pallas_call(
        functools.partial(_mixer_kernel, seq_len=seq_len),
        grid=(bsz, seq_len // tb),
        in_specs=[
            pl.BlockSpec((None, tb, D_MODEL), blk),
            pl.BlockSpec((None, tb, POOL_WIDTH), blk),
            pl.BlockSpec((None, tb, ATTN_WIDTH), blk),
            kt_spec,
            kv_spec,
            pl.BlockSpec((None, tb, ATTN_WIDTH), blk),
            _resident(bias.shape, lambda b, j: (0, 0, 0, 0)),
            _resident((D_MODEL, D_MODEL), lambda b, j: (0, 0)),
        ],
        out_specs=pl.BlockSpec((None, tb, D_MODEL), blk),
        out_shape=jax.ShapeDtypeStruct(x.shape, F32),
        scratch_shapes=[pltpu.VMEM((tb, ATTN_WIDTH), BF16)],
        compiler_params=pltpu.CompilerParams(
            dimension_semantics=("arbitrary", "arbitrary"), vmem_limit_bytes=VMEM_LIMIT),
        name="mixer",
    )(x, pool, q, kt, v, ga, bias, wout_bf)


def _attention_bias(rpb):
    col = np.arange(GRID_W)
    blocks = np.array([_KEY_BLOCKS[m] for m in range(GRID_W // 8)])[col // 8]
    kc = COL_BLOCK * blocks[:, :, None] + np.arange(COL_BLOCK)[None, None, :]
    col_start = np.clip(col - WIN_COLS // 2, 0, GRID_W - WIN_COLS)[:, None, None]
    valid = ((kc >= col_start) & (kc < col_start + WIN_COLS)).astype(np.float32)
    valid = np.broadcast_to(valid[:, :, None, :], (GRID_W, 2, WIN_ROWS, COL_BLOCK))
    valid = valid.reshape(N_COL_BLOCKS, 1, COL_BLOCK, 2 * LANES)
    valid = np.broadcast_to(valid, (N_COL_BLOCKS, 2, COL_BLOCK, 2 * LANES)).reshape(2 * GRID_W,
                                                                                    2 * LANES)
    rpb_pad = jnp.pad(rpb, ((0, 0), (0, 0), (0, LANES - rpb.shape[-1])))
    return pl.pallas_call(
        _bias_kernel,
        out_shape=jax.ShapeDtypeStruct((N_PAIRS, N_BIAS_TABLES, 2 * GRID_W, 2 * LANES), F32),
        name="bias_table",
    )(rpb_pad, jnp.asarray(valid))


def _bias_kernel(rpb_ref, valid_ref, out_ref):
    lane = lax.broadcasted_iota(jnp.int32, (8, LANES), 1)
    valid = valid_ref[...] > 0.0

    def table(p, delta, off):
        acc = {}
        for hh in range(2):
            for i in range(WIN_ROWS):
                grp = (i + off) % ROW_GROUP
                row = rpb_ref[2 * p + hh, pl.ds(i - delta + (WIN_ROWS - 1), 1), :]
                xb = jnp.broadcast_to(row, (8, LANES))
                in_grp = (lane >= grp * COL_BLOCK) & (lane < (grp + 1) * COL_BLOCK)
                for m in range(GRID_W // 8):
                    for half, n in enumerate(_KEY_BLOCKS[m]):
                        shift = (COL_BLOCK * (grp - n) + 8 * m - (WIN_COLS - 1)) % LANES
                        rolled = pltpu.roll(xb, shift, 1, stride=1, stride_axis=0)
                        key = (m, hh, half)
                        acc[key] = rolled if i == 0 else jnp.where(in_grp, rolled, acc[key])
        tile = jnp.concatenate(
            [jnp.concatenate([acc[(2 * g + mm, hh, 0)], acc[(2 * g + mm, hh, 1)]], axis=1)
             for g in range(N_COL_BLOCKS) for hh in range(2) for mm in range(2)], axis=0)
        return jnp.where(valid, tile * LOG2E, MASKED)

    def aligned(idx, carry):
        p, delta = idx // WIN_ROWS, idx % WIN_ROWS
        out_ref[p, delta] = table(p, delta, 0)
        return carry

    lax.fori_loop(0, N_PAIRS * WIN_ROWS, aligned, 0)
    for off in range(1, ROW_GROUP):
        def shifted(p, carry, off=off):
            out_ref[p, WIN_ROWS - 1 + off] = table(p, WIN_ROWS // 2, off)
            return carry

        lax.fori_loop(0, N_PAIRS, shifted, 0)


def _encoder_layer(x, norm_g, w_in, w_pool, pool_scale, q_norm_g, k_norm_g, rpb, w_out):
    bsz, seq_len, _ = x.shape
    assert seq_len % BLOCK_TOKENS == 0 and seq_len // GRID_W >= KV_ROWS
    assert seq_len % PROJ_TOKENS == 0 and PROJ_TOKENS % (ROW_GROUP * GRID_W) == 0
    assert seq_len // (ROW_GROUP * GRID_W) >= KT_GROUPS and ROW_GROUP == WIN_ROWS
    qg = jnp.tile(q_norm_g, N_HEADS)[None, :]
    kg = jnp.tile(k_norm_g, N_HEADS)[None, :]
    pool, q, kt, v, ga = _proj_call(x.reshape(bsz * seq_len, D_MODEL), seq_len, norm_g[None, :],
                                    w_in.astype(BF16), qg, kg, w_pool.astype(BF16),
                                    pool_scale[None, :])
    shp = (bsz, seq_len, 512)
    return _mixer_call(x, pool.reshape(shp), q.reshape(shp), kt, v.reshape(shp),
                       ga.reshape(shp), _attention_bias(rpb), w_out.astype(BF16))


def kernel(x_prompt, x_sample, norm_g, w_in, w_pool, pool_scale, q_norm_g, k_norm_g, rpb, w_out):
    y_prompt, y_sample = x_prompt, x_sample
    for l in range(norm_g.shape[0]):
        params = (norm_g[l], w_in[l], w_pool[l], pool_scale[l], q_norm_g[l], k_norm_g[l],
                  rpb[l], w_out[l])
        y_prompt = _encoder_layer(y_prompt, *params)
        y_sample = _encoder_layer(y_sample, *params)
    return (y_prompt, y_sample)
```

```python
import functools
import math

import jax
import jax.numpy as jnp
import numpy as np
from jax import lax
from jax.experimental import pallas as pl
from jax.experimental.pallas import tpu as pltpu

D_MODEL = 1024
POOL_WIDTH = 512
ATTN_WIDTH = 512
POOL_WINDOWS = (2, 4, 8, 16)
POOL_GROUP_DIM = POOL_WIDTH // len(POOL_WINDOWS)
HEAD_DIM = 64
N_HEADS = ATTN_WIDTH // HEAD_DIM
GRID_W = 64
WIN_ROWS = 8
WIN_COLS = 16
IN_WIDTH = 2 * POOL_WIDTH + 4 * ATTN_WIDTH
EPS = 1e-6
LOG2E = math.log2(math.e)

LANES = 128
N_PAIRS = ATTN_WIDTH // LANES
HALO = 16
MASKED = -1e30
COL_BLOCK = 16
N_COL_BLOCKS = GRID_W // COL_BLOCK
_KEY_BLOCKS = {0: (0, 1), 1: (0, 1), 2: (0, 1), 3: (1, 2), 4: (1, 2), 5: (2, 3), 6: (2, 3),
               7: (2, 3)}

PROJ_TOKENS = 1024
BLOCK_ROWS = 16
BLOCK_TOKENS = BLOCK_ROWS * GRID_W
KV_ROWS = BLOCK_ROWS + WIN_ROWS
PROJ_CHUNK_ROWS = 4
PV_LAG = 2
ROW_GROUP = 8
KT_GROUPS = BLOCK_ROWS // ROW_GROUP + 2
N_BIAS_TABLES = 2 * WIN_ROWS - 1
VMEM_LIMIT = 56 * 1024 * 1024

F32 = jnp.float32
BF16 = jnp.bfloat16


def _resident(shape, index_map):
    return pl.BlockSpec(shape, index_map, pipeline_mode=pl.Buffered(1))


def _silu(g):
    half = 0.5 * g
    return half * jnp.tanh(half) + half


def _head_rms_norm(blk, gain, low_half):
    sq = blk * blk
    s_lo = jnp.sum(jnp.where(low_half, sq, 0.0), axis=-1, keepdims=True)
    s_hi = jnp.sum(jnp.where(low_half, 0.0, sq), axis=-1, keepdims=True)
    ms = jnp.where(low_half, s_lo, s_hi) * (1.0 / HEAD_DIM)
    return blk * lax.rsqrt(ms + EPS) * gain


def _proj_kernel(x_ref, xprev_ref, xnext_ref, ng_ref, win_ref, qg_ref, kg_ref, pscale_ref,
                 pool_ref, q_ref, kt_ref, v_ref, ga_ref, uext_ref, s1_ref, s2_ref, *, seq_len):
    assert POOL_WINDOWS == (2, 4, 8, 16)
    tm = x_ref.shape[0]
    tiles_per_seq = seq_len // tm
    pos = pl.program_id(0) % tiles_per_seq

    def normed(x):
        ms = jnp.mean(x * x, axis=-1, keepdims=True)
        return (x * lax.rsqrt(ms + EPS) * ng_ref[...]).astype(BF16)

    quarter = tm // 4
    h_parts = [normed(x_ref[c * quarter:(c + 1) * quarter, :]) for c in range(4)]
    h = jnp.concatenate(h_parts, axis=0)

    def seg(i):
        return jnp.dot(h, win_ref[:, i * 512:(i + 1) * 512], preferred_element_type=F32)

    n_ext = tm + 2 * HALO
    w_u = win_ref[:, 0:POOL_WIDTH]
    first = jnp.dot(jnp.concatenate([normed(xprev_ref[...]), h_parts[0]], axis=0), w_u,
                    preferred_element_type=F32)
    uext_ref[0:HALO, :] = jnp.where(pos > 0, first[0:HALO], 0.0)
    uext_ref[HALO:HALO + quarter, :] = first[HALO:]
    for c in (1, 2):
        uext_ref[HALO + c * quarter:HALO + (c + 1) * quarter, :] = jnp.dot(
            h_parts[c], w_u, preferred_element_type=F32)
    last = jnp.dot(jnp.concatenate([h_parts[3], normed(xnext_ref[...])], axis=0), w_u,
                   preferred_element_type=F32)
    uext_ref[HALO + 3 * quarter:HALO + tm, :] = last[:quarter]
    uext_ref[HALO + tm:n_ext, :] = jnp.where(pos < tiles_per_seq - 1, last[quarter:], 0.0)
    uext_ref[n_ext:, :] = jnp.zeros((8, POOL_WIDTH), F32)

    gd = POOL_GROUP_DIM
    s1_ref[0:n_ext, gd:] = uext_ref[0:n_ext, gd:] + uext_ref[1:n_ext + 1, gd:]
    s2_ref[0:n_ext - 8, 2 * gd:] = s1_ref[0:n_ext - 8, 2 * gd:] + s1_ref[2:n_ext - 6, 2 * gd:]
    s1_ref[0:n_ext - 16, 3 * gd:] = s2_ref[0:n_ext - 16, 3 * gd:] + s2_ref[4:n_ext - 12, 3 * gd:]
    level_refs = (uext_ref, s1_ref, s2_ref, s1_ref)
    mixed = []
    for g, w in enumerate(POOL_WINDOWS):
        a = w // 2
        b = w - a
        c = slice(g * gd, (g + 1) * gd)
        half_sums = level_refs[g]
        acc = half_sums[HALO - a:HALO - a + tm, c] + half_sums[HALO:HALO + tm, c]

        def edge_mean(first):
            t = pos * tm + first + lax.broadcasted_iota(jnp.int32, (HALO, 1), 0)
            count = (jnp.minimum(t + b, seq_len) - jnp.maximum(t - a, 0)).astype(F32)
            return acc[first:first + HALO] / count

        mean = jnp.concatenate([edge_mean(0), acc[HALO:tm - HALO] * (1.0 / w),
                                edge_mean(tm - HALO)], axis=0)
        mixed.append(mean - uext_ref[HALO:HALO + tm, c])

    low_half = lax.broadcasted_iota(jnp.int32, (tm, LANES), 1) < HEAD_DIM
    q = seg(2)
    for p in range(N_PAIRS):
        c = slice(p * LANES, (p + 1) * LANES)
        qn = _head_rms_norm(q[:, c], qg_ref[:, c], low_half) * (HEAD_DIM ** -0.5 * LOG2E)
        q_ref[:, c] = qn.astype(BF16)
    k = seg(3)
    for p in range(N_PAIRS):
        c = slice(p * LANES, (p + 1) * LANES)
        kn = _head_rms_norm(k[:, c], kg_ref[:, c], low_half)
        for rg in range(tm // (ROW_GROUP * GRID_W)):
            for n in range(N_COL_BLOCKS):
                keys = jnp.concatenate(
                    [kn[(ROW_GROUP * rg + i) * GRID_W + n * COL_BLOCK:
                        (ROW_GROUP * rg + i) * GRID_W + (n + 1) * COL_BLOCK]
                     for i in range(ROW_GROUP)], axis=0)
                kt_ref[p, n, rg] = keys.T.astype(BF16)

    gate = _silu(seg(1))
    for g in range(len(POOL_WINDOWS)):
        c = slice(g * POOL_GROUP_DIM, (g + 1) * POOL_GROUP_DIM)
        pool_ref[:, c] = (mixed[g] * pscale_ref[:, c] * gate[:, c]).astype(BF16)
    ga_ref[...] = _silu(seg(5)).astype(BF16)
    v_ref[...] = seg(4).astype(BF16)


def _proj_call(x2d, seq_len, norm_g, w_in_bf, qg, kg, pscale):
    n = x2d.shape[0]
    tm = PROJ_TOKENS
    tiles_per_seq = seq_len // tm
    groups_per_tile = tm // (ROW_GROUP * GRID_W)
    halo_blocks = tm // HALO
    n_halo = n // HALO
    out = jax.ShapeDtypeStruct((n, 512), BF16)
    kt_out = jax.ShapeDtypeStruct(
        (n // seq_len, N_PAIRS, N_COL_BLOCKS, seq_len // (ROW_GROUP * GRID_W), LANES, LANES), BF16)
    kt_spec = pl.BlockSpec((None, N_PAIRS, N_COL_BLOCKS, groups_per_tile, LANES, LANES),
                           lambda i: (i // tiles_per_seq, 0, 0, i % tiles_per_seq, 0, 0))
    row = lambda i: (i, 0)
    fixed2 = lambda i: (0, 0)
    tok_spec = pl.BlockSpec((tm, 512), row)
    return pl.pallas_call(
        functools.partial(_proj_kernel, seq_len=seq_len),
        grid=(n // tm,),
        in_specs=[
            pl.BlockSpec((tm, D_MODEL), row),
            pl.BlockSpec((HALO, D_MODEL), lambda i: (jnp.maximum(i * halo_blocks - 1, 0), 0)),
            pl.BlockSpec((HALO, D_MODEL),
                         lambda i: (jnp.minimum((i + 1) * halo_blocks, n_halo - 1), 0)),
            _resident((1, D_MODEL), fixed2),
            _resident((D_MODEL, IN_WIDTH), fixed2),
            _resident((1, ATTN_WIDTH), fixed2),
            _resident((1, ATTN_WIDTH), fixed2),
            _resident((1, POOL_WIDTH), fixed2),
        ],
        out_specs=[tok_spec, tok_spec, kt_spec, tok_spec, tok_spec],
        out_shape=[out, out, kt_out, out, out],
        scratch_shapes=[pltpu.VMEM((tm + 2 * HALO + 8, POOL_WIDTH), F32)] * 3,
        compiler_params=pltpu.CompilerParams(
            dimension_semantics=("arbitrary",), vmem_limit_bytes=VMEM_LIMIT),
        name="proj",
    )(x2d, x2d, x2d, norm_g, w_in_bf, qg, kg, pscale)


def _mixer_kernel(x_ref, pool_ref, q_ref, kt_ref, v_ref, ga_ref, bias_ref, wout_ref, y_ref,
                  attn_ref, *, seq_len):
    j = pl.program_id(1)
    rows = seq_len // GRID_W
    low_half = lax.broadcasted_iota(jnp.int32, (GRID_W, LANES), 1) < HEAD_DIM
    low16 = lax.broadcasted_iota(jnp.int32, (COL_BLOCK, LANES), 1) < HEAD_DIM

    pair_cols = [slice(p * LANES, (p + 1) * LANES) for p in range(N_PAIRS)]
    zero_blk = jnp.zeros((8, LANES), F32)
    key_lane = lax.broadcasted_iota(jnp.int32, (LANES, LANES), 1).astype(F32).astype(BF16)

    kv_first_row = _kv_first_row(j, rows)
    kt_first_group = _kt_first_group(j, rows)

    def window(i):
        r = j * BLOCK_ROWS + i
        rs = jnp.clip(r - WIN_ROWS // 2, 0, rows - WIN_ROWS)
        return r - rs, rs

    def key_tiles(rs, p):
        off = rs % ROW_GROUP
        g0 = rs // ROW_GROUP - kt_first_group
        g1 = jnp.minimum(g0 + 1, KT_GROUPS - 1)
        from_first = key_lane >= (off * COL_BLOCK).astype(F32).astype(BF16)
        return [jnp.where(from_first, kt_ref[0, p, n, g0], kt_ref[0, p, n, g1])
                for n in range(N_COL_BLOCKS)]

    def value_rows(rs, c):
        off = rs % ROW_GROUP
        pieces = []
        for n in range(N_COL_BLOCKS):
            for i8 in range(ROW_GROUP):
                row = rs - kv_first_row + ((i8 - off) & (ROW_GROUP - 1))
                start = pl.multiple_of(row * GRID_W + n * COL_BLOCK, COL_BLOCK)
                pieces.append(v_ref[0, pl.ds(start, COL_BLOCK), c])
        return jnp.concatenate(pieces, axis=0)

    def scores(i):
        _, rs = window(i)
        out = []
        for p, c in enumerate(pair_cols):
            q2 = q_ref[i * GRID_W:(i + 1) * GRID_W, c]
            zero = jnp.zeros_like(q2)
            qa = jnp.where(low_half, q2, zero)
            qb = jnp.where(low_half, zero, q2)
            qq = jnp.concatenate(
                [blk[g * COL_BLOCK:(g + 1) * COL_BLOCK] for g in range(N_COL_BLOCKS)
                 for blk in (qa, qb)], axis=0)
            kt = key_tiles(rs, p)
            d0 = jnp.dot(qq[:96], jnp.concatenate(kt[:2], axis=1), preferred_element_type=F32)
            d1 = jnp.dot(qq[32:], jnp.concatenate(kt[2:], axis=1), preferred_element_type=F32)
            pieces = []
            for m8 in range(2 * N_COL_BLOCKS * 2):
                g, hh, mm = m8 // 4, (m8 // 2) % 2, m8 % 2
                row = []
                for n in _KEY_BLOCKS[2 * g + mm]:
                    src, r0, l0 = (d0, m8 * 8, n * LANES) if n < 2 else (d1, m8 * 8 - 32,
                                                                         (n - 2) * LANES)
                    row.append(src[r0:r0 + 8, l0:l0 + LANES])
                pieces.append(jnp.concatenate(row, axis=1))
            out.append(jnp.concatenate(pieces, axis=0))
        return out

    def softmax(i, s_list):
        delta, rs = window(i)
        off = rs % ROW_GROUP
        table = jnp.where(delta != WIN_ROWS // 2, delta,
                          jnp.where(off == 0, WIN_ROWS // 2, WIN_ROWS - 1 + off))
        out = []
        for p, s in enumerate(s_list):
            s = s + bias_ref[p, table]
            e = jnp.exp2(s - jnp.max(s, axis=-1, keepdims=True))
            inv_l = 1.0 / jnp.sum(e, axis=-1, keepdims=True)
            halves = []
            for t in range(2):
                rows8 = []
                for m8 in range(4 * t, 4 * t + 12):
                    blocks = _KEY_BLOCKS[2 * (m8 // 4) + m8 % 2]
                    rows8.append(jnp.concatenate(
                        [e[m8 * 8:m8 * 8 + 8, blocks.index(n) * LANES:(blocks.index(n) + 1) * LANES]
                         if n in blocks else zero_blk for n in (2 * t, 2 * t + 1)], axis=1))
                halves.append(jnp.concatenate(rows8, axis=0).astype(BF16))
            out.append((halves, inv_l))
        return out

    def weighted_values(i, e_list):
        _, rs = window(i)
        tok = slice(i * GRID_W, (i + 1) * GRID_W)
        for p, ((p0, p1), inv_l) in enumerate(e_list):
            c = pair_cols[p]
            vv = value_rows(rs, c)
            o0 = jnp.dot(p0, vv[:256], preferred_element_type=F32)
            o1 = jnp.dot(p1, vv[256:], preferred_element_type=F32)
            o = jnp.concatenate([o0[:32], o0[32:] + o1[:64], o1[64:]], axis=0) * inv_l
            o2 = jnp.concatenate(
                [jnp.where(low16, o[g * 32:g * 32 + 16], o[g * 32 + 16:g * 32 + 32])
                 for g in range(N_COL_BLOCKS)], axis=0)
            attn_ref[tok, c] = (o2 * ga_ref[tok, c].astype(F32)).astype(BF16)

    chunk_tokens = PROJ_CHUNK_ROWS * GRID_W
    n_col_chunks = D_MODEL // 512

    def proj_chunk(idx, src_ref, w_rows, first):
        tok = slice((idx // n_col_chunks) * chunk_tokens, (idx // n_col_chunks + 1) * chunk_tokens)
        cols = slice((idx % n_col_chunks) * 512, (idx % n_col_chunks + 1) * 512)
        upd = jnp.dot(src_ref[tok, :], wout_ref[w_rows, cols], preferred_element_type=F32)
        if first:
            y_ref[tok, cols] = x_ref[tok, cols] + upd
        else:
            y_ref[tok, cols] += upd

    n_chunks = (BLOCK_ROWS // PROJ_CHUNK_ROWS) * n_col_chunks
    attn_stage = [(idx // n_col_chunks + 1) * PROJ_CHUNK_ROWS + PV_LAG + idx % n_col_chunks
                  for idx in range(n_chunks)]
    last_stage = BLOCK_ROWS + PV_LAG - 1
    s_rows, e_rows = {}, {}
    for t in range(last_stage + 1):
        if t < BLOCK_ROWS:
            s_rows[t] = scores(t)
        for idx in range(n_chunks):
            if 1 + idx * BLOCK_ROWS // n_chunks == t:
                proj_chunk(idx, pool_ref, slice(0, POOL_WIDTH), True)
        if 1 <= t <= BLOCK_ROWS:
            e_rows[t - 1] = softmax(t - 1, s_rows.pop(t - 1))
        if t >= PV_LAG:
            weighted_values(t - PV_LAG, e_rows.pop(t - PV_LAG))
        for idx in range(n_chunks):
            if attn_stage[idx] == t:
                proj_chunk(idx, attn_ref, slice(POOL_WIDTH, D_MODEL), False)
    for idx in range(n_chunks):
        if attn_stage[idx] > last_stage:
            proj_chunk(idx, attn_ref, slice(POOL_WIDTH, D_MODEL), False)


def _kv_first_row(j, rows):
    return jnp.clip(j * BLOCK_ROWS - WIN_ROWS // 2, 0, rows - KV_ROWS)


def _kt_first_group(j, rows):
    return jnp.clip(j * (BLOCK_ROWS // ROW_GROUP) - 1, 0, rows // ROW_GROUP - KT_GROUPS)


def _mixer_call(x, pool, q, kt, v, ga, bias, wout_bf):
    bsz, seq_len, _ = x.shape
    tb = BLOCK_TOKENS
    rows = seq_len // GRID_W
    blk = lambda b, j: (b, j, 0)
    kt_spec = pl.BlockSpec(
        (pl.Element(1), pl.Element(N_PAIRS), pl.Element(N_COL_BLOCKS), pl.Element(KT_GROUPS),
         pl.Element(LANES), pl.Element(LANES)),
        lambda b, j: (b, 0, 0, _kt_first_group(j, rows), 0, 0))
    kv_spec = pl.BlockSpec(
        (pl.Element(1), pl.Element(KV_ROWS * GRID_W), pl.Element(ATTN_WIDTH)),
        lambda b, j: (b, _kv_first_row(j, rows) * GRID_W, 0))
    return pl.pallas_call(
        functools.partial(_mixer_kernel, seq_len=seq_len),
        grid=(bsz, seq_len // tb),
        in_specs=[
            pl.BlockSpec((None, tb, D_MODEL), blk),
            pl.BlockSpec((None, tb, POOL_WIDTH), blk),
            pl.BlockSpec((None, tb, ATTN_WIDTH), blk),
            kt_spec,
            kv_spec,
            pl.BlockSpec((None, tb, ATTN_WIDTH), blk),
            _resident(bias.shape, lambda b, j: (0, 0, 0, 0)),
            _resident((D_MODEL, D_MODEL), lambda b, j: (0, 0)),
        ],
        out_specs=pl.BlockSpec((None, tb, D_MODEL), blk),
        out_shape=jax.ShapeDtypeStruct(x.shape, F32),
        scratch_shapes=[pltpu.VMEM((tb, ATTN_WIDTH), BF16)],
        compiler_params=pltpu.CompilerParams(
            dimension_semantics=("arbitrary", "arbitrary"), vmem_limit_bytes=VMEM_LIMIT),
        name="mixer",
    )(x, pool, q, kt, v, ga, bias, wout_bf)


def _attention_bias(rpb):
    col = np.arange(GRID_W)
    blocks = np.array([_KEY_BLOCKS[m] for m in range(GRID_W // 8)])[col // 8]
    kc = COL_BLOCK * blocks[:, :, None] + np.arange(COL_BLOCK)[None, None, :]
    col_start = np.clip(col - WIN_COLS // 2, 0, GRID_W - WIN_COLS)[:, None, None]
    valid = ((kc >= col_start) & (kc < col_start + WIN_COLS)).astype(np.float32)
    valid = np.broadcast_to(valid[:, :, None, :], (GRID_W, 2, WIN_ROWS, COL_BLOCK))
    valid = valid.reshape(N_COL_BLOCKS, 1, COL_BLOCK, 2 * LANES)
    valid = np.broadcast_to(valid, (N_COL_BLOCKS, 2, COL_BLOCK, 2 * LANES)).reshape(2 * GRID_W,
                                                                                    2 * LANES)
    rpb_pad = jnp.pad(rpb, ((0, 0), (0, 0), (0, LANES - rpb.shape[-1])))
    return pl.pallas_call(
        _bias_kernel,
        out_shape=jax.ShapeDtypeStruct((N_PAIRS, N_BIAS_TABLES, 2 * GRID_W, 2 * LANES), F32),
        name="bias_table",
    )(rpb_pad, jnp.asarray(valid))


def _bias_kernel(rpb_ref, valid_ref, out_ref):
    lane = lax.broadcasted_iota(jnp.int32, (8, LANES), 1)
    valid = valid_ref[...] > 0.0

    def table(p, delta, off):
        acc = {}
        for hh in range(2):
            for i in range(WIN_ROWS):
                grp = (i + off) % ROW_GROUP
                row = rpb_ref[2 * p + hh, pl.ds(i - delta + (WIN_ROWS - 1), 1), :]
                xb = jnp.broadcast_to(row, (8, LANES))
                in_grp = (lane >= grp * COL_BLOCK) & (lane < (grp + 1) * COL_BLOCK)
                for m in range(GRID_W // 8):
                    for half, n in enumerate(_KEY_BLOCKS[m]):
                        shift = (COL_BLOCK * (grp - n) + 8 * m - (WIN_COLS - 1)) % LANES
                        rolled = pltpu.roll(xb, shift, 1, stride=1, stride_axis=0)
                        key = (m, hh, half)
                        acc[key] = rolled if i == 0 else jnp.where(in_grp, rolled, acc[key])
        tile = jnp.concatenate(
            [jnp.concatenate([acc[(2 * g + mm, hh, 0)], acc[(2 * g + mm, hh, 1)]], axis=1)
             for g in range(N_COL_BLOCKS) for hh in range(2) for mm in range(2)], axis=0)
        return jnp.where(valid, tile * LOG2E, MASKED)

    def aligned(idx, carry):
        p, delta = idx // WIN_ROWS, idx % WIN_ROWS
        out_ref[p, delta] = table(p, delta, 0)
        return carry

    lax.fori_loop(0, N_PAIRS * WIN_ROWS, aligned, 0)
    for off in range(1, ROW_GROUP):
        def shifted(p, carry, off=off):
            out_ref[p, WIN_ROWS - 1 + off] = table(p, WIN_ROWS // 2, off)
            return carry

        lax.fori_loop(0, N_PAIRS, shifted, 0)


def _fold_kernel(wu_ref, wpool_ref, out_ref):
    for g in range(len(POOL_WINDOWS)):
        c = slice(g * POOL_GROUP_DIM, (g + 1) * POOL_GROUP_DIM)
        out_ref[:, c] = jnp.dot(wu_ref[:, c], wpool_ref[g], preferred_element_type=F32,
                                precision=lax.Precision.HIGHEST)


def _fold_pool_maps(w_u, w_pool):
    return pl.pallas_call(
        _fold_kernel,
        out_shape=jax.ShapeDtypeStruct(w_u.shape, F32),
        name="fold_pool_maps",
    )(w_u, w_pool)


def _encoder_layer(x, norm_g, w_in, w_pool, pool_scale, q_norm_g, k_norm_g, rpb, w_out):
    bsz, seq_len, _ = x.shape
    assert seq_len % BLOCK_TOKENS == 0 and seq_len // GRID_W >= KV_ROWS
    assert seq_len % PROJ_TOKENS == 0 and PROJ_TOKENS % (ROW_GROUP * GRID_W) == 0
    assert seq_len // (ROW_GROUP * GRID_W) >= KT_GROUPS and ROW_GROUP == WIN_ROWS
    qg = jnp.tile(q_norm_g, N_HEADS)[None, :]
    kg = jnp.tile(k_norm_g, N_HEADS)[None, :]
    w_in_bf = jnp.concatenate([_fold_pool_maps(w_in[:, :POOL_WIDTH], w_pool).astype(BF16),
                               w_in[:, POOL_WIDTH:].astype(BF16)], axis=1)
    pool, q, kt, v, ga = _proj_call(x.reshape(bsz * seq_len, D_MODEL), seq_len, norm_g[None, :],
                                    w_in_bf, qg, kg, pool_scale[None, :])
    shp = (bsz, seq_len, 512)
    return _mixer_call(x, pool.reshape(shp), q.reshape(shp), kt, v.reshape(shp),
                       ga.reshape(shp), _attention_bias(rpb), w_out.astype(BF16))


def kernel(x_prompt, x_sample, norm_g, w_in, w_pool, pool_scale, q_norm_g, k_norm_g, rpb, w_out):
    y_prompt, y_sample = x_prompt, x_sample
    for l in range(norm_g.shape[0]):
        params = (norm_g[l], w_in[l], w_pool[l], pool_scale[l], q_norm_g[l], k_norm_g[l],
                  rpb[l], w_out[l])
        y_prompt = _encoder_layer(y_prompt, *params)
        y_sample = _encoder_layer(y_sample, *params)
    return (y_prompt, y_sample)
```

```python
import functools
import math

import jax
import jax.numpy as jnp
import numpy as np
from jax import lax
from jax.experimental import pallas as pl
from jax.experimental.pallas import tpu as pltpu

D_MODEL = 1024
POOL_WIDTH = 512
ATTN_WIDTH = 512
POOL_WINDOWS = (2, 4, 8, 16)
POOL_GROUP_DIM = POOL_WIDTH // len(POOL_WINDOWS)
HEAD_DIM = 64
N_HEADS = ATTN_WIDTH // HEAD_DIM
GRID_W = 64
WIN_ROWS = 8
WIN_COLS = 16
IN_WIDTH = 2 * POOL_WIDTH + 4 * ATTN_WIDTH
EPS = 1e-6
LOG2E = math.log2(math.e)

LANES = 128
N_PAIRS = ATTN_WIDTH // LANES
HALO = 16
MASKED = -1e30
COL_BLOCK = 16
N_COL_BLOCKS = GRID_W // COL_BLOCK
_KEY_BLOCKS = {0: (0, 1), 1: (0, 1), 2: (0, 1), 3: (1, 2), 4: (1, 2), 5: (2, 3), 6: (2, 3),
               7: (2, 3)}

PROJ_TOKENS = 1024
BLOCK_ROWS = 16
BLOCK_TOKENS = BLOCK_ROWS * GRID_W
KV_ROWS = BLOCK_ROWS + WIN_ROWS
PROJ_CHUNK_ROWS = 4
PV_LAG = 2
ROW_GROUP = 8
KT_GROUPS = BLOCK_ROWS // ROW_GROUP + 2
N_BIAS_TABLES = 2 * WIN_ROWS - 1
VMEM_LIMIT = 56 * 1024 * 1024

F32 = jnp.float32
BF16 = jnp.bfloat16


def _resident(shape, index_map):
    return pl.BlockSpec(shape, index_map, pipeline_mode=pl.Buffered(1))


def _silu(g):
    half = 0.5 * g
    return half * jnp.tanh(half) + half


def _head_rms_norm(blk, gain, low_half):
    sq = blk * blk
    s_lo = jnp.sum(jnp.where(low_half, sq, 0.0), axis=-1, keepdims=True)
    s_hi = jnp.sum(jnp.where(low_half, 0.0, sq), axis=-1, keepdims=True)
    ms = jnp.where(low_half, s_lo, s_hi) * (1.0 / HEAD_DIM)
    return blk * lax.rsqrt(ms + EPS) * gain


def _proj_kernel(x_ref, xprev_ref, xnext_ref, ng_ref, win_ref, qg_ref, kg_ref, pscale_ref,
                 pool_ref, q_ref, kt_ref, v_ref, ga_ref, uext_ref, s1_ref, s2_ref, *, seq_len):
    assert POOL_WINDOWS == (2, 4, 8, 16)
    tm = x_ref.shape[0]
    tiles_per_seq = seq_len // tm
    pos = pl.program_id(0) % tiles_per_seq

    def normed(x):
        ms = jnp.mean(x * x, axis=-1, keepdims=True)
        return (x * lax.rsqrt(ms + EPS) * ng_ref[...]).astype(BF16)

    quarter = tm // 4
    h_parts = [normed(x_ref[c * quarter:(c + 1) * quarter, :]) for c in range(4)]
    h = jnp.concatenate(h_parts, axis=0)

    def seg(i):
        return jnp.dot(h, win_ref[:, i * 512:(i + 1) * 512], preferred_element_type=F32)

    n_ext = tm + 2 * HALO
    w_u = win_ref[:, 0:POOL_WIDTH]
    first = jnp.dot(jnp.concatenate([normed(xprev_ref[...]), h_parts[0]], axis=0), w_u,
                    preferred_element_type=F32)
    uext_ref[0:HALO, :] = jnp.where(pos > 0, first[0:HALO], 0.0)
    uext_ref[HALO:HALO + quarter, :] = first[HALO:]
    for c in (1, 2):
        uext_ref[HALO + c * quarter:HALO + (c + 1) * quarter, :] = jnp.dot(
            h_parts[c], w_u, preferred_element_type=F32)
    last = jnp.dot(jnp.concatenate([h_parts[3], normed(xnext_ref[...])], axis=0), w_u,
                   preferred_element_type=F32)
    uext_ref[HALO + 3 * quarter:HALO + tm, :] = last[:quarter]
    uext_ref[HALO + tm:n_ext, :] = jnp.where(pos < tiles_per_seq - 1, last[quarter:], 0.0)
    uext_ref[n_ext:, :] = jnp.zeros((8, POOL_WIDTH), F32)

    gd = POOL_GROUP_DIM
    s1_ref[0:n_ext, gd:] = uext_ref[0:n_ext, gd:] + uext_ref[1:n_ext + 1, gd:]
    s2_ref[0:n_ext - 8, 2 * gd:] = s1_ref[0:n_ext - 8, 2 * gd:] + s1_ref[2:n_ext - 6, 2 * gd:]
    s1_ref[0:n_ext - 16, 3 * gd:] = s2_ref[0:n_ext - 16, 3 * gd:] + s2_ref[4:n_ext - 12, 3 * gd:]
    level_refs = (uext_ref, s1_ref, s2_ref, s1_ref)
    mixed = []
    for g, w in enumerate(POOL_WINDOWS):
        a = w // 2
        b = w - a
        c = slice(g * gd, (g + 1) * gd)
        half_sums = level_refs[g]
        acc = half_sums[HALO - a:HALO - a + tm, c] + half_sums[HALO:HALO + tm, c]

        def edge_mean(first):
            t = pos * tm + first + lax.broadcasted_iota(jnp.int32, (HALO, 1), 0)
            count = (jnp.minimum(t + b, seq_len) - jnp.maximum(t - a, 0)).astype(F32)
            return acc[first:first + HALO] / count

        mean = jnp.concatenate([edge_mean(0), acc[HALO:tm - HALO] * (1.0 / w),
                                edge_mean(tm - HALO)], axis=0)
        mixed.append(mean - uext_ref[HALO:HALO + tm, c])

    low_half = lax.broadcasted_iota(jnp.int32, (tm, LANES), 1) < HEAD_DIM
    q = seg(2)
    for p in range(N_PAIRS):
        c = slice(p * LANES, (p + 1) * LANES)
        qn = _head_rms_norm(q[:, c], qg_ref[:, c], low_half) * (HEAD_DIM ** -0.5 * LOG2E)
        q_ref[:, c] = qn.astype(BF16)
    k = seg(3)
    for p in range(N_PAIRS):
        c = slice(p * LANES, (p + 1) * LANES)
        kn = _head_rms_norm(k[:, c], kg_ref[:, c], low_half)
        for rg in range(tm // (ROW_GROUP * GRID_W)):
            for n in range(N_COL_BLOCKS):
                keys = jnp.concatenate(
                    [kn[(ROW_GROUP * rg + i) * GRID_W + n * COL_BLOCK:
                        (ROW_GROUP * rg + i) * GRID_W + (n + 1) * COL_BLOCK]
                     for i in range(ROW_GROUP)], axis=0)
                kt_ref[p, n, rg] = keys.T.astype(BF16)

    gate = _silu(seg(1))
    for g in range(len(POOL_WINDOWS)):
        c = slice(g * POOL_GROUP_DIM, (g + 1) * POOL_GROUP_DIM)
        pool_ref[:, c] = (mixed[g] * pscale_ref[:, c] * gate[:, c]).astype(BF16)
    ga_ref[...] = _silu(seg(5)).astype(BF16)
    v_ref[...] = seg(4).astype(BF16)


def _proj_call(x2d, seq_len, norm_g, w_in_bf, qg, kg, pscale):
    n = x2d.shape[0]
    tm = PROJ_TOKENS
    tiles_per_seq = seq_len // tm
    groups_per_tile = tm // (ROW_GROUP * GRID_W)
    halo_blocks = tm // HALO
    n_halo = n // HALO
    out = jax.ShapeDtypeStruct((n, 512), BF16)
    kt_out = jax.ShapeDtypeStruct(
        (n // seq_len, N_PAIRS, N_COL_BLOCKS, seq_len // (ROW_GROUP * GRID_W), LANES, LANES), BF16)
    kt_spec = pl.BlockSpec((None, N_PAIRS, N_COL_BLOCKS, groups_per_tile, LANES, LANES),
                           lambda i: (i // tiles_per_seq, 0, 0, i % tiles_per_seq, 0, 0))
    row = lambda i: (i, 0)
    fixed2 = lambda i: (0, 0)
    tok_spec = pl.BlockSpec((tm, 512), row)
    return pl.pallas_call(
        functools.partial(_proj_kernel, seq_len=seq_len),
        grid=(n // tm,),
        in_specs=[
            pl.BlockSpec((tm, D_MODEL), row),
            pl.BlockSpec((HALO, D_MODEL), lambda i: (jnp.maximum(i * halo_blocks - 1, 0), 0)),
            pl.BlockSpec((HALO, D_MODEL),
                         lambda i: (jnp.minimum((i + 1) * halo_blocks, n_halo - 1), 0)),
            _resident((1, D_MODEL), fixed2),
            _resident((D_MODEL, IN_WIDTH), fixed2),
            _resident((1, ATTN_WIDTH), fixed2),
            _resident((1, ATTN_WIDTH), fixed2),
            _resident((1, POOL_WIDTH), fixed2),
        ],
        out_specs=[tok_spec, tok_spec, kt_spec, tok_spec, tok_spec],
        out_shape=[out, out, kt_out, out, out],
        scratch_shapes=[pltpu.VMEM((tm + 2 * HALO + 8, POOL_WIDTH), F32)] * 3,
        compiler_params=pltpu.CompilerParams(
            dimension_semantics=("arbitrary",), vmem_limit_bytes=VMEM_LIMIT),
        name="proj",
    )(x2d, x2d, x2d, norm_g, w_in_bf, qg, kg, pscale)


def _mixer_kernel(x_ref, pool_ref, q_ref, kt_ref, v_ref, ga_ref, bias_ref, wout_ref, y_ref,
                  attn_ref, *, seq_len):
    j = pl.program_id(1)
    rows = seq_len // GRID_W
    low_half = lax.broadcasted_iota(jnp.int32, (GRID_W, LANES), 1) < HEAD_DIM
    low16 = lax.broadcasted_iota(jnp.int32, (COL_BLOCK, LANES), 1) < HEAD_DIM

    pair_cols = [slice(p * LANES, (p + 1) * LANES) for p in range(N_PAIRS)]
    zero_blk = jnp.zeros((8, LANES), F32)
    key_lane = lax.broadcasted_iota(jnp.int32, (LANES, LANES), 1).astype(F32).astype(BF16)

    kv_first_row = _kv_first_row(j, rows)
    kt_first_group = _kt_first_group(j, rows)

    def window(i):
        r = j * BLOCK_ROWS + i
        rs = jnp.clip(r - WIN_ROWS // 2, 0, rows - WIN_ROWS)
        return r - rs, rs

    def key_tiles(rs, p):
        off = rs % ROW_GROUP
        g0 = rs // ROW_GROUP - kt_first_group
        g1 = jnp.minimum(g0 + 1, KT_GROUPS - 1)
        from_first = key_lane >= (off * COL_BLOCK).astype(F32).astype(BF16)
        return [jnp.where(from_first, kt_ref[0, p, n, g0], kt_ref[0, p, n, g1])
                for n in range(N_COL_BLOCKS)]

    def value_rows(rs, c):
        off = rs % ROW_GROUP
        pieces = []
        for n in range(N_COL_BLOCKS):
            for i8 in range(ROW_GROUP):
                row = rs - kv_first_row + ((i8 - off) & (ROW_GROUP - 1))
                start = pl.multiple_of(row * GRID_W + n * COL_BLOCK, COL_BLOCK)
                pieces.append(v_ref[0, pl.ds(start, COL_BLOCK), c])
        return jnp.concatenate(pieces, axis=0)

    def scores(i):
        _, rs = window(i)
        out = []
        for p, c in enumerate(pair_cols):
            q2 = q_ref[i * GRID_W:(i + 1) * GRID_W, c]
            zero = jnp.zeros_like(q2)
            qa = jnp.where(low_half, q2, zero)
            qb = jnp.where(low_half, zero, q2)
            qq = jnp.concatenate(
                [blk[g * COL_BLOCK:(g + 1) * COL_BLOCK] for g in range(N_COL_BLOCKS)
                 for blk in (qa, qb)], axis=0)
            kt = key_tiles(rs, p)
            d0 = jnp.dot(qq[:96], jnp.concatenate(kt[:2], axis=1), preferred_element_type=F32)
            d1 = jnp.dot(qq[32:], jnp.concatenate(kt[2:], axis=1), preferred_element_type=F32)
            pieces = []
            for m8 in range(2 * N_COL_BLOCKS * 2):
                g, hh, mm = m8 // 4, (m8 // 2) % 2, m8 % 2
                row = []
                for n in _KEY_BLOCKS[2 * g + mm]:
                    src, r0, l0 = (d0, m8 * 8, n * LANES) if n < 2 else (d1, m8 * 8 - 32,
                                                                         (n - 2) * LANES)
                    row.append(src[r0:r0 + 8, l0:l0 + LANES])
                pieces.append(jnp.concatenate(row, axis=1))
            out.append(jnp.concatenate(pieces, axis=0))
        return out

    def softmax(i, s_list):
        delta, rs = window(i)
        off = rs % ROW_GROUP
        table = jnp.where(delta != WIN_ROWS // 2, delta,
                          jnp.where(off == 0, WIN_ROWS // 2, WIN_ROWS - 1 + off))
        out = []
        for p, s in enumerate(s_list):
            s = s + bias_ref[p, table]
            e = jnp.exp2(s - jnp.max(s, axis=-1, keepdims=True))
            inv_l = 1.0 / jnp.sum(e, axis=-1, keepdims=True)
            halves = []
            for t in range(2):
                rows8 = []
                for m8 in range(4 * t, 4 * t + 12):
                    blocks = _KEY_BLOCKS[2 * (m8 // 4) + m8 % 2]
                    rows8.append(jnp.concatenate(
                        [e[m8 * 8:m8 * 8 + 8, blocks.index(n) * LANES:(blocks.index(n) + 1) * LANES]
                         if n in blocks else zero_blk for n in (2 * t, 2 * t + 1)], axis=1))
                halves.append(jnp.concatenate(rows8, axis=0).astype(BF16))
            out.append((halves, inv_l))
        return out

    def weighted_values(i, e_list):
        _, rs = window(i)
        tok = slice(i * GRID_W, (i + 1) * GRID_W)
        for p, ((p0, p1), inv_l) in enumerate(e_list):
            c = pair_cols[p]
            vv = value_rows(rs, c)
            o0 = jnp.dot(p0, vv[:256], preferred_element_type=F32)
            o1 = jnp.dot(p1, vv[256:], preferred_element_type=F32)
            o = jnp.concatenate([o0[:32], o0[32:] + o1[:64], o1[64:]], axis=0) * inv_l
            o2 = jnp.concatenate(
                [jnp.where(low16, o[g * 32:g * 32 + 16], o[g * 32 + 16:g * 32 + 32])
                 for g in range(N_COL_BLOCKS)], axis=0)
            attn_ref[tok, c] = (o2 * ga_ref[tok, c].astype(F32)).astype(BF16)

    chunk_tokens = PROJ_CHUNK_ROWS * GRID_W
    n_col_chunks = D_MODEL // 512

    def proj_chunk(idx, src_ref, w_rows, first):
        tok = slice((idx // n_col_chunks) * chunk_tokens, (idx // n_col_chunks + 1) * chunk_tokens)
        cols = slice((idx % n_col_chunks) * 512, (idx % n_col_chunks + 1) * 512)
        upd = jnp.dot(src_ref[tok, :], wout_ref[w_rows, cols], preferred_element_type=F32)
        if first:
            y_ref[tok, cols] = x_ref[tok, cols] + upd
        else:
            y_ref[tok, cols] += upd

    n_chunks = (BLOCK_ROWS // PROJ_CHUNK_ROWS) * n_col_chunks
    attn_stage = [(idx // n_col_chunks + 1) * PROJ_CHUNK_ROWS + PV_LAG + idx % n_col_chunks
                  for idx in range(n_chunks)]
    last_stage = BLOCK_ROWS + PV_LAG - 1
    s_rows, e_rows = {}, {}
    for t in range(last_stage + 1):
        if t < BLOCK_ROWS:
            s_rows[t] = scores(t)
        for idx in range(n_chunks):
            if 1 + idx * BLOCK_ROWS // n_chunks == t:
                proj_chunk(idx, pool_ref, slice(0, POOL_WIDTH), True)
        if 1 <= t <= BLOCK_ROWS:
            e_rows[t - 1] = softmax(t - 1, s_rows.pop(t - 1))
        if t >= PV_LAG:
            weighted_values(t - PV_LAG, e_rows.pop(t - PV_LAG))
        for idx in range(n_chunks):
            if attn_stage[idx] == t:
                proj_chunk(idx, attn_ref, slice(POOL_WIDTH, D_MODEL), False)
    for idx in range(n_chunks):
        if attn_stage[idx] > last_stage:
            proj_chunk(idx, attn_ref, slice(POOL_WIDTH, D_MODEL), False)


def _kv_first_row(j, rows):
    return jnp.clip(j * BLOCK_ROWS - WIN_ROWS // 2, 0, rows - KV_ROWS)


def _kt_first_group(j, rows):
    return jnp.clip(j * (BLOCK_ROWS // ROW_GROUP) - 1, 0, rows // ROW_GROUP - KT_GROUPS)


def _mixer_call(x, pool, q, kt, v, ga, bias, wout_bf):
    bsz, seq_len, _ = x.shape
    tb = BLOCK_TOKENS
    rows = seq_len // GRID_W
    blk = lambda b, j: (b, j, 0)
    kt_spec = pl.BlockSpec(
        (pl.Element(1), pl.Element(N_PAIRS), pl.Element(N_COL_BLOCKS), pl.Element(KT_GROUPS),
         pl.Element(LANES), pl.Element(LANES)),
        lambda b, j: (b, 0, 0, _kt_first_group(j, rows), 0, 0))
    kv_spec = pl.BlockSpec(
        (pl.Element(1), pl.Element(KV_ROWS * GRID_W), pl.Element(ATTN_WIDTH)),
        lambda b, j: (b, _kv_first_row(j, rows) * GRID_W, 0))
    return pl.pallas_call(
        functools.partial(_mixer_kernel, seq_len=seq_len),
        grid=(bsz, seq_len // tb),
        in_specs=[
            pl.BlockSpec((None, tb, D_MODEL), blk),
            pl.BlockSpec((None, tb, POOL_WIDTH), blk),
            pl.BlockSpec((None, tb, ATTN_WIDTH), blk),
            kt_spec,
            kv_spec,
            pl.BlockSpec((None, tb, ATTN_WIDTH), blk),
            _resident(bias.shape, lambda b, j: (0, 0, 0, 0)),
            _resident((D_MODEL, D_MODEL), lambda b, j: (0, 0)),
        ],
        out_specs=pl.BlockSpec((None, tb, D_MODEL), blk),
        out_shape=jax.ShapeDtypeStruct(x.shape, F32),
        scratch_shapes=[pltpu.VMEM((tb, ATTN_WIDTH), BF16)],
        compiler_params=pltpu.CompilerParams(
            dimension_semantics=("arbitrary", "arbitrary"), vmem_limit_bytes=VMEM_LIMIT),
        name="mixer",
    )(x, pool, q, kt, v, ga, bias, wout_bf)


def _attention_bias(rpb):
    col = np.arange(GRID_W)
    blocks = np.array([_KEY_BLOCKS[m] for m in range(GRID_W // 8)])[col // 8]
    kc = COL_BLOCK * blocks[:, :, None] + np.arange(COL_BLOCK)[None, None, :]
    col_start = np.clip(col - WIN_COLS // 2, 0, GRID_W - WIN_COLS)[:, None, None]
    valid = ((kc >= col_start) & (kc < col_start + WIN_COLS)).astype(np.float32)
    valid = np.broadcast_to(valid[:, :, None, :], (GRID_W, 2, WIN_ROWS, COL_BLOCK))
    valid = valid.reshape(N_COL_BLOCKS, 1, COL_BLOCK, 2 * LANES)
    valid = np.broadcast_to(valid, (N_COL_BLOCKS, 2, COL_BLOCK, 2 * LANES)).reshape(2 * GRID_W,
                                                                                    2 * LANES)
    rpb_pad = jnp.pad(rpb, ((0, 0), (0, 0), (0, LANES - rpb.shape[-1])))
    return pl.pallas_call(
        _bias_kernel,
        out_shape=jax.ShapeDtypeStruct((N_PAIRS, N_BIAS_TABLES, 2 * GRID_W, 2 * LANES), F32),
        name="bias_table",
    )(rpb_pad, jnp.asarray(valid))


def _bias_kernel(rpb_ref, valid_ref, out_ref):
    lane = lax.broadcasted_iota(jnp.int32, (8, LANES), 1)
    valid = valid_ref[...] > 0.0

    def table(p, delta, off):
        acc = {}
        for hh in range(2):
            for i in range(WIN_ROWS):
                grp = (i + off) % ROW_GROUP
                row = rpb_ref[2 * p + hh, pl.ds(i - delta + (WIN_ROWS - 1), 1), :]
                xb = jnp.broadcast_to(row, (8, LANES))
                in_grp = (lane >= grp * COL_BLOCK) & (lane < (grp + 1) * COL_BLOCK)
                for m in range(GRID_W // 8):
                    for half, n in enumerate(_KEY_BLOCKS[m]):
                        shift = (COL_BLOCK * (grp - n) + 8 * m - (WIN_COLS - 1)) % LANES
                        rolled = pltpu.roll(xb, shift, 1, stride=1, stride_axis=0)
                        key = (m, hh, half)
                        acc[key] = rolled if i == 0 else jnp.where(in_grp, rolled, acc[key])
        tile = jnp.concatenate(
            [jnp.concatenate([acc[(2 * g + mm, hh, 0)], acc[(2 * g + mm, hh, 1)]], axis=1)
             for g in range(N_COL_BLOCKS) for hh in range(2) for mm in range(2)], axis=0)
        return jnp.where(valid, tile * LOG2E, MASKED)

    def aligned(idx, carry):
        p, delta = idx // WIN_ROWS, idx % WIN_ROWS
        out_ref[p, delta] = table(p, delta, 0)
        return carry

    lax.fori_loop(0, N_PAIRS * WIN_ROWS, aligned, 0)
    def shifted(p, carry):
        base = out_ref[p, WIN_ROWS // 2]
        for off in range(1, ROW_GROUP):
            out_ref[p, WIN_ROWS - 1 + off] = jnp.concatenate(
                [pltpu.roll(base[:, half * LANES:(half + 1) * LANES], off * COL_BLOCK, 1)
                 for half in range(2)], axis=1)
        return carry

    lax.fori_loop(0, N_PAIRS, shifted, 0)


def _weights_kernel(win_ref, wpool_ref, out_ref):
    @pl.when(pl.program_id(0) == 0)
    def _():
        for g in range(len(POOL_WINDOWS)):
            c = slice(g * POOL_GROUP_DIM, (g + 1) * POOL_GROUP_DIM)
            out_ref[:, c] = jnp.dot(win_ref[:, c], wpool_ref[g], preferred_element_type=F32,
                                    precision=lax.Precision.HIGHEST).astype(BF16)

    @pl.when(pl.program_id(0) > 0)
    def _():
        out_ref[...] = win_ref[...].astype(BF16)


def _projection_weights(w_in, w_pool):
    chunk = pl.BlockSpec((D_MODEL, POOL_WIDTH), lambda i: (0, i))
    return pl.pallas_call(
        _weights_kernel,
        grid=(IN_WIDTH // POOL_WIDTH,),
        in_specs=[chunk, _resident(w_pool.shape, lambda i: (0, 0, 0))],
        out_specs=chunk,
        out_shape=jax.ShapeDtypeStruct(w_in.shape, BF16),
        compiler_params=pltpu.CompilerParams(dimension_semantics=("arbitrary",)),
        name="projection_weights",
    )(w_in, w_pool)


def _encoder_layer(x, norm_g, w_in, w_pool, pool_scale, q_norm_g, k_norm_g, rpb, w_out):
    bsz, seq_len, _ = x.shape
    assert seq_len % BLOCK_TOKENS == 0 and seq_len // GRID_W >= KV_ROWS
    assert seq_len % PROJ_TOKENS == 0 and PROJ_TOKENS % (ROW_GROUP * GRID_W) == 0
    assert seq_len // (ROW_GROUP * GRID_W) >= KT_GROUPS and ROW_GROUP == WIN_ROWS
    qg = jnp.tile(q_norm_g, N_HEADS)[None, :]
    kg = jnp.tile(k_norm_g, N_HEADS)[None, :]
    pool, q, kt, v, ga = _proj_call(x.reshape(bsz * seq_len, D_MODEL), seq_len, norm_g[None, :],
                                    _projection_weights(w_in, w_pool), qg, kg, pool_scale[None, :])
    shp = (bsz, seq_len, 512)
    return _mixer_call(x, pool.reshape(shp), q.reshape(shp), kt, v.reshape(shp),
                       ga.reshape(shp), _attention_bias(rpb), w_out.astype(BF16))


def kernel(x_prompt, x_sample, norm_g, w_in, w_pool, pool_scale, q_norm_g, k_norm_g, rpb, w_out):
    y_prompt, y_sample = x_prompt, x_sample
    for l in range(norm_g.shape[0]):
        params = (norm_g[l], w_in[l], w_pool[l], pool_scale[l], q_norm_g[l], k_norm_g[l],
                  rpb[l], w_out[l])
        y_prompt = _encoder_layer(y_prompt, *params)
        y_sample = _encoder_layer(y_sample, *params)
    return (y_prompt, y_sample)
```

```python
import functools
import math

import jax
import jax.numpy as jnp
import numpy as np
from jax import lax
from jax.experimental import pallas as pl
from jax.experimental.pallas import tpu as pltpu

D_MODEL = 1024
POOL_WIDTH = 512
ATTN_WIDTH = 512
POOL_WINDOWS = (2, 4, 8, 16)
POOL_GROUP_DIM = POOL_WIDTH // len(POOL_WINDOWS)
HEAD_DIM = 64
N_HEADS = ATTN_WIDTH // HEAD_DIM
GRID_W = 64
WIN_ROWS = 8
WIN_COLS = 16
IN_WIDTH = 2 * POOL_WIDTH + 4 * ATTN_WIDTH
EPS = 1e-6
LOG2E = math.log2(math.e)

LANES = 128
N_PAIRS = ATTN_WIDTH // LANES
HALO = 16
MASKED = -1e30
COL_BLOCK = 16
N_COL_BLOCKS = GRID_W // COL_BLOCK
_KEY_BLOCKS = {0: (0, 1), 1: (0, 1), 2: (0, 1), 3: (1, 2), 4: (1, 2), 5: (2, 3), 6: (2, 3),
               7: (2, 3)}

PROJ_TOKENS = 1024
BLOCK_ROWS = 16
BLOCK_TOKENS = BLOCK_ROWS * GRID_W
KV_ROWS = BLOCK_ROWS + WIN_ROWS
PROJ_CHUNK_ROWS = 4
PV_LAG = 2
ROW_GROUP = 8
KT_GROUPS = BLOCK_ROWS // ROW_GROUP + 2
N_BIAS_TABLES = 2 * WIN_ROWS - 1
VMEM_LIMIT = 56 * 1024 * 1024

F32 = jnp.float32
BF16 = jnp.bfloat16


def _resident(shape, index_map):
    return pl.BlockSpec(shape, index_map, pipeline_mode=pl.Buffered(1))


def _silu(g):
    half = 0.5 * g
    return half * jnp.tanh(half) + half


def _head_rms_norm(blk, gain, low_half):
    sq = blk * blk
    s_lo = jnp.sum(jnp.where(low_half, sq, 0.0), axis=-1, keepdims=True)
    s_hi = jnp.sum(jnp.where(low_half, 0.0, sq), axis=-1, keepdims=True)
    ms = jnp.where(low_half, s_lo, s_hi) * (1.0 / HEAD_DIM)
    return blk * lax.rsqrt(ms + EPS) * gain


def _proj_kernel(x_ref, xprev_ref, xnext_ref, ng_ref, win_ref, qg_ref, kg_ref, pscale_ref,
                 pool_ref, q_ref, kt_ref, v_ref, ga_ref, uext_ref, s1_ref, s2_ref, *, seq_len):
    assert POOL_WINDOWS == (2, 4, 8, 16)
    tm = x_ref.shape[0]
    tiles_per_seq = seq_len // tm
    pos = pl.program_id(0) % tiles_per_seq

    def normed(x):
        ms = jnp.mean(x * x, axis=-1, keepdims=True)
        return (x * lax.rsqrt(ms + EPS) * ng_ref[...]).astype(BF16)

    quarter = tm // 4
    h_parts = [normed(x_ref[c * quarter:(c + 1) * quarter, :]) for c in range(4)]
    h = jnp.concatenate(h_parts, axis=0)

    def seg(i):
        return jnp.dot(h, win_ref[:, i * 512:(i + 1) * 512], preferred_element_type=F32)

    n_ext = tm + 2 * HALO
    w_u = win_ref[:, 0:POOL_WIDTH]
    first = jnp.dot(jnp.concatenate([normed(xprev_ref[...]), h_parts[0]], axis=0), w_u,
                    preferred_element_type=F32)
    uext_ref[0:HALO, :] = jnp.where(pos > 0, first[0:HALO], 0.0)
    uext_ref[HALO:HALO + quarter, :] = first[HALO:]
    for c in (1, 2):
        uext_ref[HALO + c * quarter:HALO + (c + 1) * quarter, :] = jnp.dot(
            h_parts[c], w_u, preferred_element_type=F32)
    last = jnp.dot(jnp.concatenate([h_parts[3], normed(xnext_ref[...])], axis=0), w_u,
                   preferred_element_type=F32)
    uext_ref[HALO + 3 * quarter:HALO + tm, :] = last[:quarter]
    uext_ref[HALO + tm:n_ext, :] = jnp.where(pos < tiles_per_seq - 1, last[quarter:], 0.0)
    uext_ref[n_ext:, :] = jnp.zeros((8, POOL_WIDTH), F32)

    gd = POOL_GROUP_DIM
    s1_ref[0:n_ext, gd:] = uext_ref[0:n_ext, gd:] + uext_ref[1:n_ext + 1, gd:]
    s2_ref[0:n_ext - 8, 2 * gd:] = s1_ref[0:n_ext - 8, 2 * gd:] + s1_ref[2:n_ext - 6, 2 * gd:]
    s1_ref[0:n_ext - 16, 3 * gd:] = s2_ref[0:n_ext - 16, 3 * gd:] + s2_ref[4:n_ext - 12, 3 * gd:]
    level_refs = (uext_ref, s1_ref, s2_ref, s1_ref)
    mixed = []
    for g, w in enumerate(POOL_WINDOWS):
        a = w // 2
        b = w - a
        c = slice(g * gd, (g + 1) * gd)
        half_sums = level_refs[g]
        acc = half_sums[HALO - a:HALO - a + tm, c] + half_sums[HALO:HALO + tm, c]

        def edge_mean(first):
            t = pos * tm + first + lax.broadcasted_iota(jnp.int32, (HALO, 1), 0)
            count = (jnp.minimum(t + b, seq_len) - jnp.maximum(t - a, 0)).astype(F32)
            return acc[first:first + HALO] / count

        mean = jnp.concatenate([edge_mean(0), acc[HALO:tm - HALO] * (1.0 / w),
                                edge_mean(tm - HALO)], axis=0)
        mixed.append(mean - uext_ref[HALO:HALO + tm, c])

    gate = _silu(seg(1))
    for g in range(len(POOL_WINDOWS)):
        c = slice(g * POOL_GROUP_DIM, (g + 1) * POOL_GROUP_DIM)
        pool_ref[:, c] = (mixed[g] * pscale_ref[:, c] * gate[:, c]).astype(BF16)

    low_half = lax.broadcasted_iota(jnp.int32, (tm, LANES), 1) < HEAD_DIM
    q = seg(2)
    for p in range(N_PAIRS):
        c = slice(p * LANES, (p + 1) * LANES)
        qn = _head_rms_norm(q[:, c], qg_ref[:, c], low_half) * (HEAD_DIM ** -0.5 * LOG2E)
        q_ref[:, c] = qn.astype(BF16)
    k = seg(3)
    for p in range(N_PAIRS):
        c = slice(p * LANES, (p + 1) * LANES)
        kn = _head_rms_norm(k[:, c], kg_ref[:, c], low_half)
        for rg in range(tm // (ROW_GROUP * GRID_W)):
            for n in range(N_COL_BLOCKS):
                keys = jnp.concatenate(
                    [kn[(ROW_GROUP * rg + i) * GRID_W + n * COL_BLOCK:
                        (ROW_GROUP * rg + i) * GRID_W + (n + 1) * COL_BLOCK]
                     for i in range(ROW_GROUP)], axis=0)
                kt_ref[p, n, rg] = keys.T.astype(BF16)

    ga_ref[...] = _silu(seg(5)).astype(BF16)
    v_ref[...] = seg(4).astype(BF16)


def _proj_call(x2d, seq_len, norm_g, w_in_bf, qg, kg, pscale):
    n = x2d.shape[0]
    tm = PROJ_TOKENS
    tiles_per_seq = seq_len // tm
    groups_per_tile = tm // (ROW_GROUP * GRID_W)
    halo_blocks = tm // HALO
    n_halo = n // HALO
    out = jax.ShapeDtypeStruct((n, 512), BF16)
    kt_out = jax.ShapeDtypeStruct(
        (n // seq_len, N_PAIRS, N_COL_BLOCKS, seq_len // (ROW_GROUP * GRID_W), LANES, LANES), BF16)
    kt_spec = pl.BlockSpec((None, N_PAIRS, N_COL_BLOCKS, groups_per_tile, LANES, LANES),
                           lambda i: (i // tiles_per_seq, 0, 0, i % tiles_per_seq, 0, 0))
    row = lambda i: (i, 0)
    fixed2 = lambda i: (0, 0)
    tok_spec = pl.BlockSpec((tm, 512), row)
    return pl.pallas_call(
        functools.partial(_proj_kernel, seq_len=seq_len),
        grid=(n // tm,),
        in_specs=[
            pl.BlockSpec((tm, D_MODEL), row),
            pl.BlockSpec((HALO, D_MODEL), lambda i: (jnp.maximum(i * halo_blocks - 1, 0), 0)),
            pl.BlockSpec((HALO, D_MODEL),
                         lambda i: (jnp.minimum((i + 1) * halo_blocks, n_halo - 1), 0)),
            _resident((1, D_MODEL), fixed2),
            _resident((D_MODEL, IN_WIDTH), fixed2),
            _resident((1, ATTN_WIDTH), fixed2),
            _resident((1, ATTN_WIDTH), fixed2),
            _resident((1, POOL_WIDTH), fixed2),
        ],
        out_specs=[tok_spec, tok_spec, kt_spec, tok_spec, tok_spec],
        out_shape=[out, out, kt_out, out, out],
        scratch_shapes=[pltpu.VMEM((tm + 2 * HALO + 8, POOL_WIDTH), F32)] * 3,
        compiler_params=pltpu.CompilerParams(
            dimension_semantics=("arbitrary",), vmem_limit_bytes=VMEM_LIMIT),
        name="proj",
    )(x2d, x2d, x2d, norm_g, w_in_bf, qg, kg, pscale)


def _mixer_kernel(x_ref, pool_ref, q_ref, kt_ref, v_ref, ga_ref, bias_ref, wout_ref, y_ref,
                  attn_ref, *, seq_len):
    j = pl.program_id(1)
    rows = seq_len // GRID_W
    low_half = lax.broadcasted_iota(jnp.int32, (GRID_W, LANES), 1) < HEAD_DIM
    low16 = lax.broadcasted_iota(jnp.int32, (COL_BLOCK, LANES), 1) < HEAD_DIM

    pair_cols = [slice(p * LANES, (p + 1) * LANES) for p in range(N_PAIRS)]
    zero_blk = jnp.zeros((8, LANES), F32)
    key_lane = lax.broadcasted_iota(jnp.int32, (LANES, LANES), 1).astype(F32).astype(BF16)

    kv_first_row = _kv_first_row(j, rows)
    kt_first_group = _kt_first_group(j, rows)

    def window(i):
        r = j * BLOCK_ROWS + i
        rs = jnp.clip(r - WIN_ROWS // 2, 0, rows - WIN_ROWS)
        return r - rs, rs

    def key_tiles(rs, p):
        off = rs % ROW_GROUP
        g0 = rs // ROW_GROUP - kt_first_group
        g1 = jnp.minimum(g0 + 1, KT_GROUPS - 1)
        from_first = key_lane >= (off * COL_BLOCK).astype(F32).astype(BF16)
        return [jnp.where(from_first, kt_ref[0, p, n, g0], kt_ref[0, p, n, g1])
                for n in range(N_COL_BLOCKS)]

    def value_rows(rs, c):
        off = rs % ROW_GROUP
        pieces = []
        for n in range(N_COL_BLOCKS):
            for i8 in range(ROW_GROUP):
                row = rs - kv_first_row + ((i8 - off) & (ROW_GROUP - 1))
                start = pl.multiple_of(row * GRID_W + n * COL_BLOCK, COL_BLOCK)
                pieces.append(v_ref[0, pl.ds(start, COL_BLOCK), c])
        return jnp.concatenate(pieces, axis=0)

    def scores(i):
        _, rs = window(i)
        out = []
        for p, c in enumerate(pair_cols):
            q2 = q_ref[i * GRID_W:(i + 1) * GRID_W, c]
            zero = jnp.zeros_like(q2)
            qa = jnp.where(low_half, q2, zero)
            qb = jnp.where(low_half, zero, q2)
            qq = jnp.concatenate(
                [blk[g * COL_BLOCK:(g + 1) * COL_BLOCK] for g in range(N_COL_BLOCKS)
                 for blk in (qa, qb)], axis=0)
            kt = key_tiles(rs, p)
            d0 = jnp.dot(qq[:96], jnp.concatenate(kt[:2], axis=1), preferred_element_type=F32)
            d1 = jnp.dot(qq[32:], jnp.concatenate(kt[2:], axis=1), preferred_element_type=F32)
            pieces = []
            for m8 in range(2 * N_COL_BLOCKS * 2):
                g, hh, mm = m8 // 4, (m8 // 2) % 2, m8 % 2
                row = []
                for n in _KEY_BLOCKS[2 * g + mm]:
                    src, r0, l0 = (d0, m8 * 8, n * LANES) if n < 2 else (d1, m8 * 8 - 32,
                                                                         (n - 2) * LANES)
                    row.append(src[r0:r0 + 8, l0:l0 + LANES])
                pieces.append(jnp.concatenate(row, axis=1))
            out.append(jnp.concatenate(pieces, axis=0))
        return out

    def softmax(i, s_list):
        delta, rs = window(i)
        off = rs % ROW_GROUP
        table = jnp.where(delta != WIN_ROWS // 2, delta,
                          jnp.where(off == 0, WIN_ROWS // 2, WIN_ROWS - 1 + off))
        out = []
        for p, s in enumerate(s_list):
            s = s + bias_ref[p, table]
            e = jnp.exp2(s - jnp.max(s, axis=-1, keepdims=True))
            inv_l = 1.0 / jnp.sum(e, axis=-1, keepdims=True)
            halves = []
            for t in range(2):
                rows8 = []
                for m8 in range(4 * t, 4 * t + 12):
                    blocks = _KEY_BLOCKS[2 * (m8 // 4) + m8 % 2]
                    rows8.append(jnp.concatenate(
                        [e[m8 * 8:m8 * 8 + 8, blocks.index(n) * LANES:(blocks.index(n) + 1) * LANES]
                         if n in blocks else zero_blk for n in (2 * t, 2 * t + 1)], axis=1))
                halves.append(jnp.concatenate(rows8, axis=0).astype(BF16))
            out.append((halves, inv_l))
        return out

    def weighted_values(i, e_list):
        _, rs = window(i)
        tok = slice(i * GRID_W, (i + 1) * GRID_W)
        for p, ((p0, p1), inv_l) in enumerate(e_list):
            c = pair_cols[p]
            vv = value_rows(rs, c)
            o0 = jnp.dot(p0, vv[:256], preferred_element_type=F32)
            o1 = jnp.dot(p1, vv[256:], preferred_element_type=F32)
            o = jnp.concatenate([o0[:32], o0[32:] + o1[:64], o1[64:]], axis=0) * inv_l
            o2 = jnp.concatenate(
                [jnp.where(low16, o[g * 32:g * 32 + 16], o[g * 32 + 16:g * 32 + 32])
                 for g in range(N_COL_BLOCKS)], axis=0)
            attn_ref[tok, c] = (o2 * ga_ref[tok, c].astype(F32)).astype(BF16)

    chunk_tokens = PROJ_CHUNK_ROWS * GRID_W
    n_col_chunks = D_MODEL // 512

    def proj_chunk(idx, src_ref, w_rows, first):
        tok = slice((idx // n_col_chunks) * chunk_tokens, (idx // n_col_chunks + 1) * chunk_tokens)
        cols = slice((idx % n_col_chunks) * 512, (idx % n_col_chunks + 1) * 512)
        upd = jnp.dot(src_ref[tok, :], wout_ref[w_rows, cols], preferred_element_type=F32)
        if first:
            y_ref[tok, cols] = x_ref[tok, cols] + upd
        else:
            y_ref[tok, cols] += upd

    n_chunks = (BLOCK_ROWS // PROJ_CHUNK_ROWS) * n_col_chunks
    attn_stage = [(idx // n_col_chunks + 1) * PROJ_CHUNK_ROWS + PV_LAG + idx % n_col_chunks
                  for idx in range(n_chunks)]
    last_stage = BLOCK_ROWS + PV_LAG - 1
    s_rows, e_rows = {}, {}
    for t in range(last_stage + 1):
        if t < BLOCK_ROWS:
            s_rows[t] = scores(t)
        for idx in range(n_chunks):
            if 1 + idx * BLOCK_ROWS // n_chunks == t:
                proj_chunk(idx, pool_ref, slice(0, POOL_WIDTH), True)
        if 1 <= t <= BLOCK_ROWS:
            e_rows[t - 1] = softmax(t - 1, s_rows.pop(t - 1))
        if t >= PV_LAG:
            weighted_values(t - PV_LAG, e_rows.pop(t - PV_LAG))
        for idx in range(n_chunks):
            if attn_stage[idx] == t:
                proj_chunk(idx, attn_ref, slice(POOL_WIDTH, D_MODEL), False)
    for idx in range(n_chunks):
        if attn_stage[idx] > last_stage:
            proj_chunk(idx, attn_ref, slice(POOL_WIDTH, D_MODEL), False)


def _kv_first_row(j, rows):
    return jnp.clip(j * BLOCK_ROWS - WIN_ROWS // 2, 0, rows - KV_ROWS)


def _kt_first_group(j, rows):
    return jnp.clip(j * (BLOCK_ROWS // ROW_GROUP) - 1, 0, rows // ROW_GROUP - KT_GROUPS)


def _mixer_call(x, pool, q, kt, v, ga, bias, wout_bf):
    bsz, seq_len, _ = x.shape
    tb = BLOCK_TOKENS
    rows = seq_len // GRID_W
    blk = lambda b, j: (b, j, 0)
    kt_spec = pl.BlockSpec(
        (pl.Element(1), pl.Element(N_PAIRS), pl.Element(N_COL_BLOCKS), pl.Element(KT_GROUPS),
         pl.Element(LANES), pl.Element(LANES)),
        lambda b, j: (b, 0, 0, _kt_first_group(j, rows), 0, 0))
    kv_spec = pl.BlockSpec(
        (pl.Element(1), pl.Element(KV_ROWS * GRID_W), pl.Element(ATTN_WIDTH)),
        lambda b, j: (b, _kv_first_row(j, rows) * GRID_W, 0))
    return pl.pallas_call(
        functools.partial(_mixer_kernel, seq_len=seq_len),
        grid=(bsz, seq_len // tb),
        in_specs=[
            pl.BlockSpec((None, tb, D_MODEL), blk),
            pl.BlockSpec((None, tb, POOL_WIDTH), blk),
            pl.BlockSpec((None, tb, ATTN_WIDTH), blk),
            kt_spec,
            kv_spec,
            pl.BlockSpec((None, tb, ATTN_WIDTH), blk),
            _resident(bias.shape, lambda b, j: (0, 0, 0, 0)),
            _resident((D_MODEL, D_MODEL), lambda b, j: (0, 0)),
        ],
        out_specs=pl.BlockSpec((None, tb, D_MODEL), blk),
        out_shape=jax.ShapeDtypeStruct(x.shape, F32),
        scratch_shapes=[pltpu.VMEM((tb, ATTN_WIDTH), BF16)],
        compiler_params=pltpu.CompilerParams(
            dimension_semantics=("arbitrary", "arbitrary"), vmem_limit_bytes=VMEM_LIMIT),
        name="mixer",
    )(x, pool, q, kt, v, ga, bias, wout_bf)


def _attention_bias(rpb):
    col = np.arange(GRID_W)
    blocks = np.array([_KEY_BLOCKS[m] for m in range(GRID_W // 8)])[col // 8]
    kc = COL_BLOCK * blocks[:, :, None] + np.arange(COL_BLOCK)[None, None, :]
    col_start = np.clip(col - WIN_COLS // 2, 0, GRID_W - WIN_COLS)[:, None, None]
    valid = ((kc >= col_start) & (kc < col_start + WIN_COLS)).astype(np.float32)
    valid = np.broadcast_to(valid[:, :, None, :], (GRID_W, 2, WIN_ROWS, COL_BLOCK))
    valid = valid.reshape(N_COL_BLOCKS, 1, COL_BLOCK, 2 * LANES)
    valid = np.broadcast_to(valid, (N_COL_BLOCKS, 2, COL_BLOCK, 2 * LANES)).reshape(2 * GRID_W,
                                                                                    2 * LANES)
    rpb_pad = jnp.pad(rpb, ((0, 0), (0, 0), (0, LANES - rpb.shape[-1])))
    return pl.pallas_call(
        _bias_kernel,
        out_shape=jax.ShapeDtypeStruct((N_PAIRS, N_BIAS_TABLES, 2 * GRID_W, 2 * LANES), F32),
        name="bias_table",
    )(rpb_pad, jnp.asarray(valid))


def _bias_kernel(rpb_ref, valid_ref, out_ref):
    lane = lax.broadcasted_iota(jnp.int32, (8, LANES), 1)
    valid = valid_ref[...] > 0.0

    def table(p, delta, off):
        acc = {}
        for hh in range(2):
            for i in range(WIN_ROWS):
                grp = (i + off) % ROW_GROUP
                row = rpb_ref[2 * p + hh, pl.ds(i - delta + (WIN_ROWS - 1), 1), :]
                xb = jnp.broadcast_to(row, (8, LANES))
                in_grp = (lane >= grp * COL_BLOCK) & (lane < (grp + 1) * COL_BLOCK)
                for m in range(GRID_W // 8):
                    for half, n in enumerate(_KEY_BLOCKS[m]):
                        shift = (COL_BLOCK * (grp - n) + 8 * m - (WIN_COLS - 1)) % LANES
                        rolled = pltpu.roll(xb, shift, 1, stride=1, stride_axis=0)
                        key = (m, hh, half)
                        acc[key] = rolled if i == 0 else jnp.where(in_grp, rolled, acc[key])
        tile = jnp.concatenate(
            [jnp.concatenate([acc[(2 * g + mm, hh, 0)], acc[(2 * g + mm, hh, 1)]], axis=1)
             for g in range(N_COL_BLOCKS) for hh in range(2) for mm in range(2)], axis=0)
        return jnp.where(valid, tile * LOG2E, MASKED)

    def aligned(idx, carry):
        p, delta = idx // WIN_ROWS, idx % WIN_ROWS
        out_ref[p, delta] = table(p, delta, 0)
        return carry

    lax.fori_loop(0, N_PAIRS * WIN_ROWS, aligned, 0)
    def shifted(p, carry):
        base = out_ref[p, WIN_ROWS // 2]
        for off in range(1, ROW_GROUP):
            out_ref[p, WIN_ROWS - 1 + off] = jnp.concatenate(
                [pltpu.roll(base[:, half * LANES:(half + 1) * LANES], off * COL_BLOCK, 1)
                 for half in range(2)], axis=1)
        return carry

    lax.fori_loop(0, N_PAIRS, shifted, 0)


def _weights_kernel(win_ref, wpool_ref, out_ref):
    @pl.when(pl.program_id(0) == 0)
    def _():
        for g in range(len(POOL_WINDOWS)):
            c = slice(g * POOL_GROUP_DIM, (g + 1) * POOL_GROUP_DIM)
            out_ref[:, c] = jnp.dot(win_ref[:, c], wpool_ref[g], preferred_element_type=F32,
                                    precision=lax.Precision.HIGHEST).astype(BF16)

    @pl.when(pl.program_id(0) > 0)
    def _():
        out_ref[...] = win_ref[...].astype(BF16)


def _projection_weights(w_in, w_pool):
    chunk = pl.BlockSpec((D_MODEL, POOL_WIDTH), lambda i: (0, i))
    return pl.pallas_call(
        _weights_kernel,
        grid=(IN_WIDTH // POOL_WIDTH,),
        in_specs=[chunk, _resident(w_pool.shape, lambda i: (0, 0, 0))],
        out_specs=chunk,
        out_shape=jax.ShapeDtypeStruct(w_in.shape, BF16),
        compiler_params=pltpu.CompilerParams(dimension_semantics=("arbitrary",)),
        name="projection_weights",
    )(w_in, w_pool)


def _encoder_layer(x, norm_g, w_in, w_pool, pool_scale, q_norm_g, k_norm_g, rpb, w_out):
    bsz, seq_len, _ = x.shape
    assert seq_len % BLOCK_TOKENS == 0 and seq_len // GRID_W >= KV_ROWS
    assert seq_len % PROJ_TOKENS == 0 and PROJ_TOKENS % (ROW_GROUP * GRID_W) == 0
    assert seq_len // (ROW_GROUP * GRID_W) >= KT_GROUPS and ROW_GROUP == WIN_ROWS
    qg = jnp.tile(q_norm_g, N_HEADS)[None, :]
    kg = jnp.tile(k_norm_g, N_HEADS)[None, :]
    pool, q, kt, v, ga = _proj_call(x.reshape(bsz * seq_len, D_MODEL), seq_len, norm_g[None, :],
                                    _projection_weights(w_in, w_pool), qg, kg, pool_scale[None, :])
    shp = (bsz, seq_len, 512)
    return _mixer_call(x, pool.reshape(shp), q.reshape(shp), kt, v.reshape(shp),
                       ga.reshape(shp), _attention_bias(rpb), w_out.astype(BF16))


def kernel(x_prompt, x_sample, norm_g, w_in, w_pool, pool_scale, q_norm_g, k_norm_g, rpb, w_out):
    y_prompt, y_sample = x_prompt, x_sample
    for l in range(norm_g.shape[0]):
        params = (norm_g[l], w_in[l], w_pool[l], pool_scale[l], q_norm_g[l], k_norm_g[l],
                  rpb[l], w_out[l])
        y_prompt = _encoder_layer(y_prompt, *params)
        y_sample = _encoder_layer(y_sample, *params)
    return (y_prompt, y_sample)
```
